```python
import math
import jax, jax.numpy as jnp
from jax import lax
import numpy as np

D_MODEL = 1024
BATCH = 8
SEQ = 2048
DEPTH = 2

N_A_LAYERS = DEPTH // 2
N_B_LAYERS = DEPTH - N_A_LAYERS

SSM_EXPAND = 2
D_INNER = SSM_EXPAND * D_MODEL
SSM_HEAD_DIM = 64
SSM_HEADS = D_INNER // SSM_HEAD_DIM
SSM_GROUPS = 4
SSM_STATE = 128
CONV_WIDTH = 4
CHUNK = 128
GN = SSM_GROUPS * SSM_STATE
CONV_DIM = D_INNER + 2 * GN
IN_PROJ_DIM = D_INNER + CONV_DIM + SSM_HEADS

ATT_HEAD_DIM = 64
N_Q_HEADS = D_MODEL // ATT_HEAD_DIM
N_KV_HEADS = 4
WINDOW = 128
ROPE_THETA = 10000.0

D_FF = 2816
FFN_RES_WEIGHT = 0.5
EPS = 1e-5

kernel_name = 'yoco_ssd_swa_sink_macaron'


def rmsnorm(x, w):
    xf = x.astype(jnp.float32)
    xf = xf * lax.rsqrt(jnp.mean(xf * xf, axis=-1, keepdims=True) + EPS)
    return (xf * w.astype(jnp.float32)).astype(x.dtype)


def swiglu(h, w_gate, w_up, w_down):
    return (jax.nn.silu(h @ w_gate) * (h @ w_up)) @ w_down


def rope_tables(seqlen):
    pos = jnp.arange(seqlen, dtype=jnp.float32)
    inv = 1.0 / (ROPE_THETA ** (jnp.arange(0, ATT_HEAD_DIM, 2, dtype=jnp.float32) / ATT_HEAD_DIM))
    ang = pos[:, None] * inv[None, :]
    return jnp.cos(ang), jnp.sin(ang)


def apply_rope(t, cos, sin):
    tf = t.astype(jnp.float32)
    t1, t2 = jnp.split(tf, 2, axis=-1)
    c = cos[:, None, :]
    s = sin[:, None, :]
    return jnp.concatenate([t1 * c - t2 * s, t2 * c + t1 * s], axis=-1).astype(t.dtype)


def causal_depthwise_conv(u, w, b):
    out = lax.conv_general_dilated(
        u, w[:, None, :].astype(u.dtype), window_strides=(1,),
        padding=[(CONV_WIDTH - 1, 0)],
        dimension_numbers=('NWC', 'WIO', 'NWC'),
        feature_group_count=u.shape[-1])
    return out + b


def segsum(a):
    cs = jnp.cumsum(a, axis=-1)
    diff = cs[..., :, None] - cs[..., None, :]
    t = a.shape[-1]
    mask = jnp.tril(jnp.ones((t, t), dtype=bool))
    return jnp.where(mask, diff, -jnp.inf)


def ssd_chunked(xdt, a, b_ssm, c_ssm):
    bsz, seqlen, _, _ = xdt.shape
    nc = seqlen // CHUNK
    r = SSM_HEADS // SSM_GROUPS
    x = xdt.reshape(bsz, nc, CHUNK, SSM_GROUPS, r, SSM_HEAD_DIM)
    a = a.reshape(bsz, nc, CHUNK, SSM_GROUPS, r).transpose(0, 3, 4, 1, 2)
    bc = b_ssm.reshape(bsz, nc, CHUNK, SSM_GROUPS, SSM_STATE)
    cc = c_ssm.reshape(bsz, nc, CHUNK, SSM_GROUPS, SSM_STATE)
    a_cs = jnp.cumsum(a, axis=-1)
    decay_in = jnp.exp(segsum(a))
    cb = jnp.einsum('bclgn,bcsgn->bcgls', cc, bc)
    y_diag = jnp.einsum('bcgls,bgrcls,bcsgrp->bclgrp', cb, decay_in, x)
    decay_states = jnp.exp(a_cs[..., -1:] - a_cs)
    states = jnp.einsum('bclgn,bgrcl,bclgrp->bcgrpn', bc, decay_states, x)
    chunk_decay = jnp.exp(a_cs[..., -1])
    states_c = jnp.moveaxis(states, 1, 0)
    decay_c = jnp.moveaxis(chunk_decay, 3, 0)

    def step(carry, inp):
        s, d = inp
        return carry * d[..., None, None] + s, carry

    _, prev = lax.scan(step, jnp.zeros_like(states_c[0]), (states_c, decay_c))
    prev = jnp.moveaxis(prev, 0, 1)
    decay_out = jnp.exp(a_cs)
    y_off = jnp.einsum('bclgn,bcgrpn,bgrcl->bclgrp', cc, prev, decay_out)
    return (y_diag + y_off).reshape(bsz, seqlen, SSM_HEADS, SSM_HEAD_DIM)


def mamba2_mixer(h, w_in, conv_w, conv_b, dt_bias, a_log, d_skip, norm_w, w_out):
    bsz, seqlen, _ = h.shape
    zxbcdt = h @ w_in
    z = zxbcdt[..., :D_INNER]
    xbc = zxbcdt[..., D_INNER:D_INNER + CONV_DIM]
    dt_raw = zxbcdt[..., D_INNER + CONV_DIM:]
    xbc = jax.nn.silu(causal_depthwise_conv(xbc, conv_w, conv_b))
    xs = xbc[..., :D_INNER].reshape(bsz, seqlen, SSM_HEADS, SSM_HEAD_DIM).astype(jnp.float32)
    b_ssm = xbc[..., D_INNER:D_INNER + GN].reshape(bsz, seqlen, SSM_GROUPS, SSM_STATE).astype(jnp.float32)
    c_ssm = xbc[..., D_INNER + GN:].reshape(bsz, seqlen, SSM_GROUPS, SSM_STATE).astype(jnp.float32)
    dt = jax.nn.softplus(dt_raw.astype(jnp.float32) + dt_bias.astype(jnp.float32))
    a = -jnp.exp(a_log.astype(jnp.float32))
    y = ssd_chunked(xs * dt[..., None], dt * a, b_ssm, c_ssm)
    y = y + xs * d_skip.astype(jnp.float32)[:, None]
    y = y.reshape(bsz, seqlen, D_INNER) * jax.nn.silu(z.astype(jnp.float32))
    yg = y.reshape(bsz, seqlen, SSM_GROUPS, D_INNER // SSM_GROUPS)
    yg = yg * lax.rsqrt(jnp.mean(yg * yg, axis=-1, keepdims=True) + EPS)
    y = (yg.reshape(bsz, seqlen, D_INNER) * norm_w.astype(jnp.float32)).astype(h.dtype)
    return y @ w_out


def shared_kv(x, kv_norm_w, w_k, b_k, w_v, b_v, cos, sin):
    bsz, seqlen, _ = x.shape
    hkv = rmsnorm(x, kv_norm_w)
    k = (hkv @ w_k + b_k).reshape(bsz, seqlen, N_KV_HEADS, ATT_HEAD_DIM)
    v = (hkv @ w_v + b_v).reshape(bsz, seqlen, N_KV_HEADS, ATT_HEAD_DIM)
    return apply_rope(k, cos, sin), v


def band_blocks(t):
    prev = jnp.pad(t[:, :-1], ((0, 0), (1, 0), (0, 0), (0, 0), (0, 0)))
    return jnp.concatenate([prev, t], axis=2)


def swa_sink_attention(h, k_rot, v, w_q, b_q, sinks, w_o, b_o, cos, sin):
    bsz, seqlen, _ = h.shape
    nb = seqlen // WINDOW
    grp = N_Q_HEADS // N_KV_HEADS
    q = (h @ w_q + b_q).reshape(bsz, seqlen, N_Q_HEADS, ATT_HEAD_DIM)
    q = apply_rope(q, cos, sin).reshape(bsz, nb, WINDOW, N_KV_HEADS, grp, ATT_HEAD_DIM)
    k_band = band_blocks(k_rot.reshape(bsz, nb, WINDOW, N_KV_HEADS, ATT_HEAD_DIM))
    v_band = band_blocks(v.reshape(bsz, nb, WINDOW, N_KV_HEADS, ATT_HEAD_DIM))
    scale = 1.0 / math.sqrt(ATT_HEAD_DIM)
    scores = jnp.einsum('bnqkgd,bnskd->bnkgqs', q, k_band,
                        preferred_element_type=jnp.float32) * scale
    qpos = jnp.arange(WINDOW)[:, None] + WINDOW
    kpos = jnp.arange(2 * WINDOW)[None, :]
    in_window = (kpos <= qpos) & (kpos > qpos - WINDOW)
    has_prev = (jnp.arange(nb) > 0)[:, None, None] | (kpos >= WINDOW)[None]
    mask = in_window[None] & has_prev
    scores = jnp.where(mask[None, :, None, None], scores, -jnp.inf)
    sink = sinks.astype(jnp.float32).reshape(N_KV_HEADS, grp)[None, None, :, :, None]
    m = jnp.maximum(scores.max(axis=-1), sink)
    p = jnp.exp(scores - m[..., None])
    probs = p / (p.sum(axis=-1) + jnp.exp(sink - m))[..., None]
    out = jnp.einsum('bnkgqs,bnskd->bnqkgd', probs.astype(v.dtype), v_band)
    out = out.reshape(bsz, seqlen, N_Q_HEADS * ATT_HEAD_DIM)
    return out @ w_o + b_o


def setup_inputs(seed: int = 0) -> dict:
    key = jax.random.key(seed)
    ks = jax.random.split(key, 24)
    f32 = jnp.float32

    def nrm(k, shape, scale):
        return jax.random.normal(k, shape, f32) * scale

    dt0 = jnp.exp(jax.random.uniform(ks[8], (N_A_LAYERS, SSM_HEADS), f32,
                                     math.log(1e-3), math.log(1e-1)))
    return {
        'x': nrm(ks[0], (BATCH, SEQ, D_MODEL), 1.0),
        'norm_w': 1.0 + nrm(ks[1], (DEPTH, 3, D_MODEL), 0.01),
        'ffn_w_gate': nrm(ks[2], (DEPTH, 2, D_MODEL, D_FF), D_MODEL ** -0.5),
        'ffn_w_up': nrm(ks[3], (DEPTH, 2, D_MODEL, D_FF), D_MODEL ** -0.5),
        'ffn_w_down': nrm(ks[4], (DEPTH, 2, D_FF, D_MODEL), D_FF ** -0.5),
        'ssm_w_in': nrm(ks[5], (N_A_LAYERS, D_MODEL, IN_PROJ_DIM), D_MODEL ** -0.5),
        'ssm_conv_w': nrm(ks[6], (N_A_LAYERS, CONV_WIDTH, CONV_DIM), CONV_WIDTH ** -0.5),
        'ssm_conv_b': nrm(ks[7], (N_A_LAYERS, CONV_DIM), 0.01),
        'ssm_dt_bias': dt0 + jnp.log(-jnp.expm1(-dt0)),
        'ssm_a_log': jnp.log(jax.random.uniform(ks[9], (N_A_LAYERS, SSM_HEADS), f32, 1.0, 16.0)),
        'ssm_d': 1.0 + nrm(ks[10], (N_A_LAYERS, SSM_HEADS), 0.01),
        'ssm_norm_w': 1.0 + nrm(ks[11], (N_A_LAYERS, D_INNER), 0.01),
        'ssm_w_out': nrm(ks[12], (N_A_LAYERS, D_INNER, D_MODEL), D_INNER ** -0.5),
        'kv_norm_w': 1.0 + nrm(ks[13], (D_MODEL,), 0.01),
        'w_k': nrm(ks[14], (D_MODEL, N_KV_HEADS * ATT_HEAD_DIM), D_MODEL ** -0.5),
        'b_k': nrm(ks[15], (N_KV_HEADS * ATT_HEAD_DIM,), 0.01),
        'w_v': nrm(ks[16], (D_MODEL, N_KV_HEADS * ATT_HEAD_DIM), D_MODEL ** -0.5),
        'b_v': nrm(ks[17], (N_KV_HEADS * ATT_HEAD_DIM,), 0.01),
        'attn_w_q': nrm(ks[18], (N_B_LAYERS, D_MODEL, N_Q_HEADS * ATT_HEAD_DIM), D_MODEL ** -0.5),
        'attn_b_q': nrm(ks[19], (N_B_LAYERS, N_Q_HEADS * ATT_HEAD_DIM), 0.01),
        'attn_sinks': nrm(ks[20], (N_B_LAYERS, N_Q_HEADS), 0.5),
        'attn_w_o': nrm(ks[21], (N_B_LAYERS, N_Q_HEADS * ATT_HEAD_DIM, D_MODEL),
                        (N_Q_HEADS * ATT_HEAD_DIM) ** -0.5),
        'attn_b_o': nrm(ks[22], (N_B_LAYERS, D_MODEL), 0.01),
        'final_norm_w': 1.0 + nrm(ks[23], (D_MODEL,), 0.01),
    }


def reference(x, norm_w, ffn_w_gate, ffn_w_up, ffn_w_down,
              ssm_w_in, ssm_conv_w, ssm_conv_b, ssm_dt_bias, ssm_a_log, ssm_d, ssm_norm_w, ssm_w_out,
              kv_norm_w, w_k, b_k, w_v, b_v,
              attn_w_q, attn_b_q, attn_sinks, attn_w_o, attn_b_o,
              final_norm_w):
    cos, sin = rope_tables(x.shape[1])
    k_shared = None
    v_shared = None
    for layer in range(DEPTH):
        if layer == N_A_LAYERS:
            k_shared, v_shared = shared_kv(x, kv_norm_w, w_k, b_k, w_v, b_v, cos, sin)
        x = x + FFN_RES_WEIGHT * swiglu(rmsnorm(x, norm_w[layer, 0]), ffn_w_gate[layer, 0],
                                        ffn_w_up[layer, 0], ffn_w_down[layer, 0])
        h = rmsnorm(x, norm_w[layer, 1])
        if layer < N_A_LAYERS:
            i = layer
            x = x + mamba2_mixer(h, ssm_w_in[i], ssm_conv_w[i], ssm_conv_b[i], ssm_dt_bias[i],
                                 ssm_a_log[i], ssm_d[i], ssm_norm_w[i], ssm_w_out[i])
        else:
            j = layer - N_A_LAYERS
            x = x + swa_sink_attention(h, k_shared, v_shared, attn_w_q[j], attn_b_q[j],
                                       attn_sinks[j], attn_w_o[j], attn_b_o[j], cos, sin)
        x = x + FFN_RES_WEIGHT * swiglu(rmsnorm(x, norm_w[layer, 2]), ffn_w_gate[layer, 1],
                                        ffn_w_up[layer, 1], ffn_w_down[layer, 1])
    return rmsnorm(x, final_norm_w)
```

```python
import functools
import math

import jax
import jax.numpy as jnp
from jax import lax
from jax.experimental import pallas as pl
from jax.experimental.pallas import tpu as pltpu

F32 = jnp.float32
BF16 = jnp.bfloat16

SSM_HEAD_DIM = 64
SSM_GROUPS = 4
SSM_STATE = 128
CONV_WIDTH = 4
CHUNK = 128
ATT_HEAD_DIM = 64
N_KV_HEADS = 4
WINDOW = 128
ROPE_THETA = 10000.0
FFN_RES_WEIGHT = 0.5
EPS = 1e-5

LANES = 128
SUBLANES = 8
VMEM_LIMIT_BYTES = 56 * 1024 * 1024

TOKEN_TILE = 512


def _rmsnorm(x, w):
    return x * lax.rsqrt(jnp.mean(x * x, axis=-1, keepdims=True) + EPS) * w


def _silu(x):
    return x / (1.0 + jnp.exp(-x))


def _softplus(x):
    return jnp.maximum(x, 0.0) + jnp.log(1.0 + jnp.exp(-jnp.abs(x)))


def _resident(shape):
    zeros = (0,) * len(shape)
    return pl.BlockSpec(shape, lambda *_: zeros, pipeline_mode=pl.Buffered(1))


def _params(semantics):
    return pltpu.CompilerParams(dimension_semantics=semantics, vmem_limit_bytes=VMEM_LIMIT_BYTES)


def _ffn_body(x_ref, nw_ref, wg_ref, wu_ref, wd_ref, *rest, final_norm):
    o_ref = rest[-1]
    x = x_ref[...]
    h = _rmsnorm(x, nw_ref[...]).astype(BF16)
    g = jnp.dot(h, wg_ref[...], preferred_element_type=F32)
    u = jnp.dot(h, wu_ref[...], preferred_element_type=F32)
    a = (_silu(g) * u).astype(BF16)
    y = x + FFN_RES_WEIGHT * jnp.dot(a, wd_ref[...], preferred_element_type=F32)
    if final_norm:
        y = _rmsnorm(y, rest[0][...])
    o_ref[...] = y


def _ffn(x, nw, wg, wu, wd, final_nw=None):
    t, d = x.shape
    f = wg.shape[1]
    tm = TOKEN_TILE
    row = pl.BlockSpec((tm, d), lambda i: (i, 0))
    in_specs = [row, _resident((1, d)), _resident((d, f)), _resident((d, f)), _resident((f, d))]
    args = [x, nw.reshape(1, d), wg, wu, wd]
    if final_nw is not None:
        in_specs.append(_resident((1, d)))
        args.append(final_nw.reshape(1, d))
    return pl.pallas_call(
        functools.partial(_ffn_body, final_norm=final_nw is not None),
        grid=(t // tm,),
        in_specs=in_specs,
        out_specs=row,
        out_shape=jax.ShapeDtypeStruct((t, d), F32),
        compiler_params=_params(("parallel",)),
        name="ffn",
    )(*args)


def _ssm_in_body(x_ref, nw_ref, wz_ref, wxbc_ref, wdt_ref, cw_ref, cb_ref,
                 z_ref, xs_ref, b_ref, c_ref, dt_ref, buf_ref, *, tiles_per_seq, d_inner, gn):
    tm = x_ref.shape[0]
    halo = SUBLANES

    @pl.when(pl.program_id(0) % tiles_per_seq == 0)
    def _():
        buf_ref[0:halo, :] = jnp.zeros((halo, buf_ref.shape[1]), F32)

    h = _rmsnorm(x_ref[...], nw_ref[...]).astype(BF16)
    z_ref[...] = jnp.dot(h, wz_ref[...], preferred_element_type=F32)
    dt_ref[...] = jnp.dot(h, wdt_ref[...], preferred_element_type=F32)
    buf_ref[halo:halo + tm, :] = jnp.dot(h, wxbc_ref[...], preferred_element_type=F32)

    acc = cb_ref[...] + cw_ref[CONV_WIDTH - 1:CONV_WIDTH, :] * buf_ref[halo:halo + tm, :]
    for k in range(CONV_WIDTH - 1):
        shift = CONV_WIDTH - 1 - k
        acc = acc + cw_ref[k:k + 1, :] * buf_ref[halo - shift:halo - shift + tm, :]
    act = _silu(acc)
    xs_ref[...] = act[:, :d_inner]
    b_ref[...] = act[:, d_inner:d_inner + gn].astype(BF16)
    c_ref[...] = act[:, d_inner + gn:].astype(BF16)
    buf_ref[0:halo, :] = buf_ref[tm:tm + halo, :]


def _ssm_in(x, nw, wz, wxbc, wdt, conv_w, conv_b, seqlen):
    t, d = x.shape
    d_inner = wz.shape[1]
    conv_dim = wxbc.shape[1]
    gn = (conv_dim - d_inner) // 2
    tm = TOKEN_TILE
    row = lambda n: pl.BlockSpec((tm, n), lambda i: (i, 0))
    return pl.pallas_call(
        functools.partial(_ssm_in_body, tiles_per_seq=seqlen // tm, d_inner=d_inner, gn=gn),
        grid=(t // tm,),
        in_specs=[row(d), _resident((1, d)), _resident((d, d_inner)), _resident((d, conv_dim)),
                  _resident((d, LANES)), _resident((CONV_WIDTH, conv_dim)), _resident((1, conv_dim))],
        out_specs=[row(d_inner), row(d_inner), row(gn), row(gn), row(LANES)],
        out_shape=[jax.ShapeDtypeStruct((t, d_inner), F32), jax.ShapeDtypeStruct((t, d_inner), F32),
                   jax.ShapeDtypeStruct((t, gn), BF16), jax.ShapeDtypeStruct((t, gn), BF16),
                   jax.ShapeDtypeStruct((t, LANES), F32)],
        scratch_shapes=[pltpu.VMEM((tm + SUBLANES, conv_dim), F32)],
        compiler_params=_params(("arbitrary",)),
        name="ssm_in",
    )(x, nw.reshape(1, d), wz, wxbc, wdt, conv_w, conv_b.reshape(1, conv_dim))


def _cumsum_rows(x):
    n = x.shape[0]
    row = lax.broadcasted_iota(jnp.int32, x.shape, 0)
    step = 1
    while step < n:
        x = x + jnp.where(row >= step, pltpu.roll(x, step, axis=0), 0.0)
        step *= 2
    return x


def _expand_heads(v, e):
    hi = v.astype(BF16)
    r1 = v - hi.astype(F32)
    mid = r1.astype(BF16)
    lo = (r1 - mid.astype(F32)).astype(BF16)
    dot = lambda p: jnp.dot(p, e, preferred_element_type=F32)
    return dot(hi) + dot(mid) + dot(lo)


def _ssd_body(z_ref, xs_ref, b_ref, c_ref, dt_ref, x_ref, dtb_ref, alog_ref, dskip_ref, nw_ref, e_ref,
              wout_ref, o_ref, state_ref):
    L = CHUNK
    n_state = SSM_STATE
    pair = 2 * SSM_HEAD_DIM
    d_inner = xs_ref.shape[1]
    n_pairs = d_inner // pair
    pairs_per_group = n_pairs // SSM_GROUPS
    group_w = d_inner // SSM_GROUPS

    @pl.when(pl.program_id(1) == 0)
    def _():
        state_ref[...] = jnp.zeros_like(state_ref)

    dt = _softplus(dt_ref[...] + dtb_ref[...])
    a = dt * (-jnp.exp(alog_ref[...]))
    acs = _cumsum_rows(a)
    acs_t = acs.T
    dt_t = dt.T
    e_out = _expand_heads(jnp.exp(acs), e_ref[...])
    w_state_t = jnp.exp(acs_t[:, L - 1:L] - acs_t) * dt_t

    row = lax.broadcasted_iota(jnp.int32, (L, L), 0)
    col = lax.broadcasted_iota(jnp.int32, (L, L), 1)
    causal = col <= row
    lane = lax.broadcasted_iota(jnp.int32, (L, pair), 1)
    low_half = lane < SSM_HEAD_DIM

    cb, b_t = [], []
    for g in range(SSM_GROUPS):
        bg = b_ref[:, g * n_state:(g + 1) * n_state]
        cg = c_ref[:, g * n_state:(g + 1) * n_state]
        cb.append(lax.dot_general(cg, bg, (((1,), (1,)), ((), ())), preferred_element_type=F32))
        b_t.append(bg.astype(F32).T)

    y_diag, d_state = [], []
    for j in range(n_pairs):
        g = j // pairs_per_group
        m_blocks, bt_blocks = [], []
        for h in (2 * j, 2 * j + 1):
            seg = acs[:, h:h + 1] - acs_t[h:h + 1, :]
            decay = jnp.exp(jnp.where(causal, seg, -jnp.inf))
            m_blocks.append((cb[g] * decay * dt_t[h:h + 1, :]).astype(BF16))
            bt_blocks.append((b_t[g] * w_state_t[h:h + 1, :]).astype(BF16))
        lhs = jnp.concatenate([jnp.concatenate(m_blocks, axis=1), jnp.concatenate(bt_blocks, axis=1)], axis=0)
        xb = xs_ref[:, j * pair:(j + 1) * pair]
        rhs = jnp.concatenate([jnp.where(low_half, xb, 0.0), jnp.where(low_half, 0.0, xb)], axis=0).astype(BF16)
        res = jnp.dot(lhs, rhs, preferred_element_type=F32)
        y_diag.append(res[:L])
        d_state.append(res[L:])

    state = state_ref[...]
    y_off = [jnp.dot(c_ref[:, g * n_state:(g + 1) * n_state],
                     state[:, g * group_w:(g + 1) * group_w].astype(BF16), preferred_element_type=F32)
             for g in range(SSM_GROUPS)]
    state_ref[...] = state * e_out[L - 1:L, :] + jnp.concatenate(d_state, axis=1)

    xs = xs_ref[...]
    y = jnp.concatenate(y_diag, axis=1) + jnp.concatenate(y_off, axis=1) * e_out + xs * dskip_ref[...]
    y = y * _silu(z_ref[...])
    normed = []
    for g in range(SSM_GROUPS):
        yg = y[:, g * group_w:(g + 1) * group_w]
        normed.append(yg * lax.rsqrt(jnp.mean(yg * yg, axis=-1, keepdims=True) + EPS))
    yn = (jnp.concatenate(normed, axis=1) * nw_ref[...]).astype(BF16)
    o_ref[...] = x_ref[...] + jnp.dot(yn, wout_ref[...], preferred_element_type=F32)


def _ssd(z, xs, b, c, dt_raw, x, dt_bias, a_log, d_skip, norm_w, w_out, batch, seqlen):
    t, d_inner = xs.shape
    d = x.shape[1]
    gn = b.shape[1]
    heads = d_inner // SSM_HEAD_DIM
    nc = seqlen // CHUNK
    pad = lambda v: jnp.pad(v.reshape(1, heads), ((0, 0), (0, LANES - heads)))
    expand = (jnp.arange(LANES)[:, None] == (jnp.arange(d_inner) // SSM_HEAD_DIM)[None, :]).astype(BF16)
    row = lambda n: pl.BlockSpec((CHUNK, n), lambda bi, ci: (bi * nc + ci, 0))
    return pl.pallas_call(
        _ssd_body,
        grid=(batch, nc),
        in_specs=[row(d_inner), row(d_inner), row(gn), row(gn), row(LANES), row(d),
                  _resident((1, LANES)), _resident((1, LANES)), _resident((1, d_inner)),
                  _resident((1, d_inner)), _resident((LANES, d_inner)), _resident((d_inner, d))],
        out_specs=row(d),
        out_shape=jax.ShapeDtypeStruct((t, d), F32),
        scratch_shapes=[pltpu.VMEM((SSM_STATE, d_inner), F32)],
        compiler_params=_params(("arbitrary", "arbitrary")),
        name="ssd",
    )(z, xs, b, c, dt_raw, x, pad(dt_bias), pad(a_log), jnp.repeat(d_skip, SSM_HEAD_DIM).reshape(1, d_inner),
      norm_w.reshape(1, d_inner), expand, w_out)


def _rope_tables(seqlen):
    pos = jnp.arange(seqlen, dtype=F32)
    inv = 1.0 / (ROPE_THETA ** (jnp.arange(0, ATT_HEAD_DIM, 2, dtype=F32) / ATT_HEAD_DIM))
    ang = pos[:, None] * inv[None, :]
    cos, sin = jnp.cos(ang), jnp.sin(ang)
    reps = LANES // ATT_HEAD_DIM
    return (jnp.tile(jnp.concatenate([cos, cos], axis=-1), (1, reps)),
            jnp.tile(jnp.concatenate([-sin, sin], axis=-1), (1, reps)))


def _rope(t, cos, sin_signed):
    half = ATT_HEAD_DIM // 2
    lane = lax.broadcasted_iota(jnp.int32, (t.shape[0], LANES), 1)
    first_half = (lane % ATT_HEAD_DIM) < half
    out = []
    for j in range(t.shape[1] // LANES):
        blk = t[:, j * LANES:(j + 1) * LANES]
        partner = jnp.where(first_half, pltpu.roll(blk, LANES - half, axis=1), pltpu.roll(blk, half, axis=1))
        out.append(blk * cos + partner * sin_signed)
    return jnp.concatenate(out, axis=1)


def _kv_body(x_ref, nw_ref, wk_ref, bk_ref, wv_ref, bv_ref, cos_ref, sin_ref, k_ref, v_ref):
    h = _rmsnorm(x_ref[...], nw_ref[...]).astype(BF16)
    k = jnp.dot(h, wk_ref[...], preferred_element_type=F32) + bk_ref[...]
    v = jnp.dot(h, wv_ref[...], preferred_element_type=F32) + bv_ref[...]
    k_ref[...] = _rope(k, cos_ref[...], sin_ref[...]).astype(BF16)
    v_ref[...] = v.astype(BF16)


def _kv(x, nw, wk, bk, wv, bv, cos, sin, seqlen):
    t, d = x.shape
    n = wk.shape[1]
    tm = TOKEN_TILE
    spt = seqlen // tm
    row = lambda w: pl.BlockSpec((tm, w), lambda i: (i, 0))
    table = pl.BlockSpec((tm, LANES), lambda i: (i % spt, 0))
    return pl.pallas_call(
        _kv_body,
        grid=(t // tm,),
        in_specs=[row(d), _resident((1, d)), _resident((d, n)), _resident((1, n)), _resident((d, n)),
                  _resident((1, n)), table, table],
        out_specs=[row(n), row(n)],
        out_shape=[jax.ShapeDtypeStruct((t, n), BF16), jax.ShapeDtypeStruct((t, n), BF16)],
        compiler_params=_params(("parallel",)),
        name="kv",
    )(x, nw.reshape(1, d), wk, bk.reshape(1, n), wv, bv.reshape(1, n), cos, sin)


def _q_body(x_ref, nw_ref, wq_ref, bq_ref, cos_ref, sin_ref, q_ref, *, scale):
    h = _rmsnorm(x_ref[...], nw_ref[...]).astype(BF16)
    q = jnp.dot(h, wq_ref[...], preferred_element_type=F32) + bq_ref[...]
    q_ref[...] = (_rope(q, cos_ref[...], sin_ref[...]) * scale).astype(BF16)


def _qproj(x, nw, wq, bq, cos, sin, seqlen):
    t, d = x.shape
    n = wq.shape[1]
    tm = TOKEN_TILE
    spt = seqlen // tm
    row = lambda w: pl.BlockSpec((tm, w), lambda i: (i, 0))
    table = pl.BlockSpec((tm, LANES), lambda i: (i % spt, 0))
    return pl.pallas_call(
        functools.partial(_q_body, scale=1.0 / math.sqrt(ATT_HEAD_DIM)),
        grid=(t // tm,),
        in_specs=[row(d), _resident((1, d)), _resident((d, n)), _resident((1, n)), table, table],
        out_specs=row(n),
        out_shape=jax.ShapeDtypeStruct((t, n), BF16),
        compiler_params=_params(("parallel",)),
        name="qproj",
    )(x, nw.reshape(1, d), wq, bq.reshape(1, n), cos, sin)


def _attn_body(sink_ref, q_ref, kc_ref, kp_ref, vc_ref, vp_ref, x_ref, wo_ref, bo_ref, o_ref):
    W = WINDOW
    n_blocks = q_ref.shape[1] // LANES
    blocks_per_kv = n_blocks // N_KV_HEADS
    first_key = jnp.where(pl.program_id(1) > 0, 0, W)
    row = lax.broadcasted_iota(jnp.int32, (W, 2 * W), 0)
    col = lax.broadcasted_iota(jnp.int32, (W, 2 * W), 1)
    mask = (col <= row + W) & (col > row) & (col >= first_key)
    low_half = lax.broadcasted_iota(jnp.int32, (W, LANES), 1) < ATT_HEAD_DIM

    out = []
    for kh in range(N_KV_HEADS):
        ks = slice(kh * LANES, (kh + 1) * LANES)
        k_band = jnp.concatenate([kp_ref[:, ks], kc_ref[:, ks]], axis=0)
        v_band = jnp.concatenate([vp_ref[:, ks], vc_ref[:, ks]], axis=0)
        for jj in range(blocks_per_kv):
            j = kh * blocks_per_kv + jj
            qb = q_ref[:, j * LANES:(j + 1) * LANES].astype(F32)
            lhs = jnp.concatenate([jnp.where(low_half, qb, 0.0), jnp.where(low_half, 0.0, qb)],
                                  axis=0).astype(BF16)
            s = lax.dot_general(lhs, k_band, (((1,), (1,)), ((), ())), preferred_element_type=F32)
            probs = []
            for t in range(2):
                sink = sink_ref[2 * j + t]
                sh = jnp.where(mask, s[t * W:(t + 1) * W], -jnp.inf)
                m = jnp.maximum(jnp.max(sh, axis=-1, keepdims=True), sink)
                p = jnp.exp(sh - m)
                denom = jnp.sum(p, axis=-1, keepdims=True) + jnp.exp(sink - m)
                probs.append((p / denom).astype(BF16))
            o = jnp.dot(jnp.concatenate(probs, axis=0), v_band, preferred_element_type=F32)
            out.append(jnp.where(low_half, o[:W], o[W:]))
    att = jnp.concatenate(out, axis=1).astype(BF16)
    o_ref[...] = x_ref[...] + jnp.dot(att, wo_ref[...], preferred_element_type=F32) + bo_ref[...]


def _attn(sinks, q, k, v, x, wo, bo, batch, seqlen):
    t, d = x.shape
    nq = q.shape[1]
    nkv = k.shape[1]
    nb = seqlen // WINDOW
    cur = lambda w: pl.BlockSpec((WINDOW, w), lambda bi, ni: (bi * nb + ni, 0))
    prev = lambda w: pl.BlockSpec((WINDOW, w), lambda bi, ni: (bi * nb + jnp.maximum(ni - 1, 0), 0))
    return pl.pallas_call(
        _attn_body,
        grid=(batch, nb),
        in_specs=[pl.BlockSpec(memory_space=pltpu.SMEM), cur(nq), cur(nkv), prev(nkv), cur(nkv), prev(nkv),
                  cur(d), _resident((nq, d)), _resident((1, d))],
        out_specs=cur(d),
        out_shape=jax.ShapeDtypeStruct((t, d), F32),
        compiler_params=_params(("parallel", "parallel")),
        name="attn",
    )(sinks, q, k, k, v, v, x, wo, bo.reshape(1, d))


def _dup_heads(w):
    lead = w.shape[:-1]
    w = w.reshape(*lead, N_KV_HEADS, 1, ATT_HEAD_DIM)
    return jnp.broadcast_to(w, (*lead, N_KV_HEADS, LANES // ATT_HEAD_DIM, ATT_HEAD_DIM)).reshape(*lead, -1)


def kernel(x, norm_w, ffn_w_gate, ffn_w_up, ffn_w_down, ssm_w_in, ssm_conv_w, ssm_conv_b, ssm_dt_bias, ssm_a_log, ssm_d, ssm_norm_w, ssm_w_out, kv_norm_w, w_k, b_k, w_v, b_v, attn_w_q, attn_b_q, attn_sinks, attn_w_o, attn_b_o, final_norm_w):
    batch, seqlen, d = x.shape
    assert seqlen % TOKEN_TILE == 0 and seqlen % CHUNK == 0 and seqlen % WINDOW == 0
    assert ssm_w_in.shape[0] == 1 and attn_w_q.shape[0] == 1 and norm_w.shape[0] == 2
    d_inner = ssm_w_out.shape[1]
    heads = d_inner // SSM_HEAD_DIM
    conv_dim = ssm_conv_w.shape[-1]
    bf = lambda w: w.astype(BF16)
    xt = x.reshape(batch * seqlen, d)

    def ffn(v, layer, idx, final_nw=None):
        return _ffn(v, norm_w[layer, 2 * idx], bf(ffn_w_gate[layer, idx]), bf(ffn_w_up[layer, idx]),
                    bf(ffn_w_down[layer, idx]), final_nw)

    xt = ffn(xt, 0, 0)
    w_in = ssm_w_in[0]
    w_dt = jnp.pad(w_in[:, d_inner + conv_dim:], ((0, 0), (0, LANES - heads)))
    z, xs, b, c, dt_raw = _ssm_in(xt, norm_w[0, 1], bf(w_in[:, :d_inner]), bf(w_in[:, d_inner:d_inner + conv_dim]),
                                  bf(w_dt), ssm_conv_w[0], ssm_conv_b[0], seqlen)
    xt = _ssd(z, xs, b, c, dt_raw, xt, ssm_dt_bias[0], ssm_a_log[0], ssm_d[0], ssm_norm_w[0], bf(ssm_w_out[0]),
              batch, seqlen)
    xt = ffn(xt, 0, 1)

    cos, sin = _rope_tables(seqlen)
    k, v = _kv(xt, kv_norm_w, bf(_dup_heads(w_k)), _dup_heads(b_k), bf(_dup_heads(w_v)), _dup_heads(b_v),
               cos, sin, seqlen)

    xt = ffn(xt, 1, 0)
    q = _qproj(xt, norm_w[1, 1], bf(attn_w_q[0]), attn_b_q[0], cos, sin, seqlen)
    xt = _attn(attn_sinks[0], q, k, v, xt, bf(attn_w_o[0]), attn_b_o[0], batch, seqlen)
    xt = ffn(xt, 1, 1, final_norm_w)
    return xt.reshape(batch, seqlen, d)
```

```python
import functools
import math

import jax
import jax.numpy as jnp
from jax import lax
from jax.experimental import pallas as pl
from jax.experimental.pallas import tpu as pltpu

F32 = jnp.float32
BF16 = jnp.bfloat16

SSM_HEAD_DIM = 64
SSM_GROUPS = 4
SSM_STATE = 128
CONV_WIDTH = 4
CHUNK = 128
ATT_HEAD_DIM = 64
N_KV_HEADS = 4
WINDOW = 128
ROPE_THETA = 10000.0
FFN_RES_WEIGHT = 0.5
EPS = 1e-5

LANES = 128
SUBLANES = 8
VMEM_LIMIT_BYTES = 56 * 1024 * 1024

TOKEN_TILE = 512
CONV_COL_BLOCK = 512
CONV_ROW_STRIP = 32
SEQS_PER_STEP = 2
Q_BLOCKS_PER_STEP = 2


def _rmsnorm(x, w):
    return x * lax.rsqrt(jnp.mean(x * x, axis=-1, keepdims=True) + EPS) * w


def _silu(x):
    return x / (1.0 + jnp.exp(-x))


def _softplus(x):
    return jnp.maximum(x, 0.0) + jnp.log(1.0 + jnp.exp(-jnp.abs(x)))


def _resident(shape):
    zeros = (0,) * len(shape)
    return pl.BlockSpec(shape, lambda *_: zeros, pipeline_mode=pl.Buffered(1))


def _params(semantics):
    return pltpu.CompilerParams(dimension_semantics=semantics, vmem_limit_bytes=VMEM_LIMIT_BYTES)


def _rope_tables(seqlen):
    pos = jnp.arange(seqlen, dtype=F32)
    inv = 1.0 / (ROPE_THETA ** (jnp.arange(0, ATT_HEAD_DIM, 2, dtype=F32) / ATT_HEAD_DIM))
    ang = pos[:, None] * inv[None, :]
    cos, sin = jnp.cos(ang), jnp.sin(ang)
    reps = LANES // ATT_HEAD_DIM
    return (jnp.tile(jnp.concatenate([cos, cos], axis=-1), (1, reps)),
            jnp.tile(jnp.concatenate([-sin, sin], axis=-1), (1, reps)))


def _lane_blocks(t):
    return [t[:, j * LANES:(j + 1) * LANES] for j in range(t.shape[1] // LANES)]


def _rope(t, cos, sin_signed):
    half = ATT_HEAD_DIM // 2
    lane = lax.broadcasted_iota(jnp.int32, (t.shape[0], LANES), 1)
    first_half = (lane % ATT_HEAD_DIM) < half
    out = []
    for blk in _lane_blocks(t):
        partner = jnp.where(first_half, pltpu.roll(blk, LANES - half, axis=1), pltpu.roll(blk, half, axis=1))
        out.append(blk * cos + partner * sin_signed)
    return jnp.concatenate(out, axis=1)


def _separate_heads(t):
    low_half = lax.broadcasted_iota(jnp.int32, (t.shape[0], LANES), 1) < ATT_HEAD_DIM
    out = []
    for blk in _lane_blocks(t):
        swapped = pltpu.roll(blk, ATT_HEAD_DIM, axis=1)
        out += [jnp.where(low_half, blk, 0.0), jnp.where(low_half, 0.0, swapped),
                jnp.where(low_half, swapped, 0.0), jnp.where(low_half, 0.0, blk)]
    return jnp.concatenate(out, axis=1)


def _block_body(*refs, has_pre, has_pre_bias, post, final_norm, scale):
    it = iter(refs)
    x = next(it)[...]
    if has_pre:
        a_ref, wp_ref = next(it), next(it)
        x = x + jnp.dot(a_ref[...], wp_ref[...], preferred_element_type=F32)
        if has_pre_bias:
            x = x + next(it)[...]
    nw_ref, wg_ref, wu_ref, wd_ref = next(it), next(it), next(it), next(it)
    h = _rmsnorm(x, nw_ref[...]).astype(BF16)
    g = jnp.dot(h, wg_ref[...], preferred_element_type=F32)
    u = jnp.dot(h, wu_ref[...], preferred_element_type=F32)
    a = (_silu(g) * u).astype(BF16)
    y = x + FFN_RES_WEIGHT * jnp.dot(a, wd_ref[...], preferred_element_type=F32)
    if post == "kv":
        pnw_ref, wk_ref, bk_ref, wv_ref, bv_ref, cos_ref, sin_ref = (next(it) for _ in range(7))
        hp = _rmsnorm(y, pnw_ref[...]).astype(BF16)
        k = jnp.dot(hp, wk_ref[...], preferred_element_type=F32) + bk_ref[...]
        v = jnp.dot(hp, wv_ref[...], preferred_element_type=F32) + bv_ref[...]
        k = _rope(k, cos_ref[...], sin_ref[...])
    elif post == "q":
        pnw_ref, wq_ref, bq_ref, cos_ref, sin_ref = (next(it) for _ in range(5))
        hp = _rmsnorm(y, pnw_ref[...]).astype(BF16)
        q = jnp.dot(hp, wq_ref[...], preferred_element_type=F32) + bq_ref[...]
        q = _rope(q, cos_ref[...], sin_ref[...]) * scale
    if final_norm:
        y = _rmsnorm(y, next(it)[...])
    next(it)[...] = y
    if post == "kv":
        next(it)[...] = _separate_heads(k).astype(BF16)
        next(it)[...] = _separate_heads(v).astype(BF16)
    elif post == "q":
        next(it)[...] = q.astype(BF16)


def _block(x, nw, wg, wu, wd, *, pre=None, post=None, final_nw=None, seqlen=None):
    t, d = x.shape
    f = wg.shape[1]
    tm = TOKEN_TILE
    row = lambda w: pl.BlockSpec((tm, w), lambda i: (i, 0))
    vec = lambda v: v.reshape(1, -1)
    args, in_specs = [x], [row(d)]

    def add_resident(*arrays):
        for arr in arrays:
            args.append(arr)
            in_specs.append(_resident(arr.shape))

    if pre is not None:
        a, wp, bp = pre
        args.append(a)
        in_specs.append(row(a.shape[1]))
        add_resident(wp)
        if bp is not None:
            add_resident(vec(bp))
    add_resident(vec(nw), wg, wu, wd)
    out_specs = [row(d)]
    out_shape = [jax.ShapeDtypeStruct((t, d), F32)]
    kind = None
    if post is not None:
        kind = post[0]
        table = pl.BlockSpec((tm, LANES), lambda i: (i % (seqlen // tm), 0))
        if kind == "kv":
            _, pnw, wk, bk, wv, bv, cos, sin = post
            add_resident(vec(pnw), wk, vec(bk), wv, vec(bv))
            sep = 2 * (LANES // ATT_HEAD_DIM) * wk.shape[1]
            out_specs += [row(sep), row(sep)]
            out_shape += [jax.ShapeDtypeStruct((t, sep), BF16)] * 2
        else:
            _, pnw, wq, bq, cos, sin = post
            add_resident(vec(pnw), wq, vec(bq))
            out_specs.append(row(wq.shape[1]))
            out_shape.append(jax.ShapeDtypeStruct((t, wq.shape[1]), BF16))
        args += [cos, sin]
        in_specs += [table, table]
    if final_nw is not None:
        add_resident(vec(final_nw))
    body = functools.partial(_block_body, has_pre=pre is not None, has_pre_bias=pre is not None and pre[2] is not None,
                             post=kind, final_norm=final_nw is not None, scale=1.0 / math.sqrt(ATT_HEAD_DIM))
    return pl.pallas_call(
        body,
        grid=(t // tm,),
        in_specs=in_specs,
        out_specs=out_specs,
        out_shape=out_shape,
        compiler_params=_params(("parallel",)),
        name="block",
    )(*args)


def _ssm_in_body(x_ref, nw_ref, wz_ref, wxbc_ref, wdt_ref, cw_ref, cb_ref,
                 z_ref, xs_ref, b_ref, c_ref, dt_ref, halo_ref, *, tiles_per_seq, d_inner, gn):
    tm = x_ref.shape[0]
    conv_dim = wxbc_ref.shape[1]
    cblk = CONV_COL_BLOCK
    n_blk = conv_dim // cblk

    @pl.when(pl.program_id(0) % tiles_per_seq == 0)
    def _():
        halo_ref[...] = jnp.zeros_like(halo_ref)

    h = _rmsnorm(x_ref[...], nw_ref[...]).astype(BF16)

    def project(blk):
        return jnp.dot(h, wxbc_ref[:, blk * cblk:(blk + 1) * cblk], preferred_element_type=F32)

    def conv(u, blk):
        cs = slice(blk * cblk, (blk + 1) * cblk)
        lo = blk * cblk
        for r in range(0, tm, CONV_ROW_STRIP):
            before = halo_ref[:, cs] if r == 0 else u[r - SUBLANES:r]
            ext = jnp.concatenate([before, u[r:r + CONV_ROW_STRIP]], axis=0)
            acc = cb_ref[:, cs] + cw_ref[CONV_WIDTH - 1:CONV_WIDTH, cs] * ext[SUBLANES:]
            for shift in range(1, CONV_WIDTH):
                k = CONV_WIDTH - 1 - shift
                acc = acc + cw_ref[k:k + 1, cs] * pltpu.roll(ext, shift, axis=0)[SUBLANES:]
            act = _silu(acc)
            rows = slice(r, r + CONV_ROW_STRIP)
            if lo < d_inner:
                xs_ref[rows, lo:lo + cblk] = act
            elif lo < d_inner + gn:
                b_ref[rows, lo - d_inner:lo - d_inner + cblk] = act.astype(BF16)
            else:
                c_ref[rows, lo - d_inner - gn:lo - d_inner - gn + cblk] = act.astype(BF16)
        halo_ref[:, cs] = u[tm - SUBLANES:]

    u_next = project(0)
    for blk in range(n_blk):
        u = u_next
        if blk + 1 < n_blk:
            u_next = project(blk + 1)
        else:
            z_ref[...] = jnp.dot(h, wz_ref[...], preferred_element_type=F32)
            dt_ref[...] = jnp.dot(h, wdt_ref[...], preferred_element_type=F32)
        conv(u, blk)


def _ssm_in(x, nw, wz, wxbc, wdt, conv_w, conv_b, seqlen):
    t, d = x.shape
    d_inner = wz.shape[1]
    conv_dim = wxbc.shape[1]
    gn = (conv_dim - d_inner) // 2
    assert d_inner % CONV_COL_BLOCK == 0 and gn % CONV_COL_BLOCK == 0
    tm = TOKEN_TILE
    row = lambda n: pl.BlockSpec((tm, n), lambda i: (i, 0))
    return pl.pallas_call(
        functools.partial(_ssm_in_body, tiles_per_seq=seqlen // tm, d_inner=d_inner, gn=gn),
        grid=(t // tm,),
        in_specs=[row(d), _resident((1, d)), _resident((d, d_inner)), _resident((d, conv_dim)),
                  _resident((d, LANES)), _resident((CONV_WIDTH, conv_dim)), _resident((1, conv_dim))],
        out_specs=[row(d_inner), row(d_inner), row(gn), row(gn), row(LANES)],
        out_shape=[jax.ShapeDtypeStruct((t, d_inner), F32), jax.ShapeDtypeStruct((t, d_inner), F32),
                   jax.ShapeDtypeStruct((t, gn), BF16), jax.ShapeDtypeStruct((t, gn), BF16),
                   jax.ShapeDtypeStruct((t, LANES), F32)],
        scratch_shapes=[pltpu.VMEM((SUBLANES, conv_dim), F32)],
        compiler_params=_params(("arbitrary",)),
        name="ssm_in",
    )(x, nw.reshape(1, d), wz, wxbc, wdt, conv_w, conv_b.reshape(1, conv_dim))


def _cumsum_rows(x):
    n = x.shape[0]
    row = lax.broadcasted_iota(jnp.int32, x.shape, 0)
    step = 1
    while step < n:
        x = x + jnp.where(row >= step, pltpu.roll(x, step, axis=0), 0.0)
        step *= 2
    return x


def _expand_heads(v, e, heads):
    valid = lax.broadcasted_iota(jnp.int32, v.shape, 1) < heads
    hi = v.astype(BF16).astype(F32)
    r1 = v - hi
    mid = r1.astype(BF16).astype(F32)
    lo = (r1 - mid).astype(BF16).astype(F32)
    keep = lambda p: jnp.where(valid, p, 0.0)
    packed = keep(hi) + pltpu.roll(keep(mid), heads, axis=1) + pltpu.roll(keep(lo), 2 * heads, axis=1)
    return jnp.dot(packed.astype(BF16), e, preferred_element_type=F32)


def _ssd_chunk(z, xs_ref, b, c, dt_raw, dt_bias, a_log, d_skip, norm_w, e, state_ref, seq):
    L = CHUNK
    n_state = SSM_STATE
    pair = 2 * SSM_HEAD_DIM
    d_inner = xs_ref.shape[-1]
    heads = d_inner // SSM_HEAD_DIM
    n_pairs = d_inner // pair
    pairs_per_group = n_pairs // SSM_GROUPS
    group_w = d_inner // SSM_GROUPS

    dt = _softplus(dt_raw + dt_bias)
    a = dt * (-jnp.exp(a_log))
    acs = _cumsum_rows(a)
    acs_t = acs.T
    dt_t = dt.T
    e_out = _expand_heads(jnp.exp(acs), e, heads)
    w_state_t = jnp.exp(acs_t[:, L - 1:L] - acs_t) * dt_t

    row = lax.broadcasted_iota(jnp.int32, (L, L), 0)
    col = lax.broadcasted_iota(jnp.int32, (L, L), 1)
    causal = col <= row
    low_half = lax.broadcasted_iota(jnp.int32, (L, pair), 1) < SSM_HEAD_DIM

    cb, b_t = [], []
    for g in range(SSM_GROUPS):
        bg = b[:, g * n_state:(g + 1) * n_state]
        cg = c[:, g * n_state:(g + 1) * n_state]
        cb.append(lax.dot_general(cg, bg, (((1,), (1,)), ((), ())), preferred_element_type=F32))
        b_t.append(bg.astype(F32).T)

    y_diag, d_state = [], []
    for j in range(n_pairs):
        g = j // pairs_per_group
        m_blocks, bt_blocks = [], []
        for h in (2 * j, 2 * j + 1):
            seg = acs[:, h:h + 1] - acs_t[h:h + 1, :]
            decay = jnp.exp(jnp.where(causal, seg, -jnp.inf))
            m_blocks.append((cb[g] * decay * dt_t[h:h + 1, :]).astype(BF16))
            bt_blocks.append((b_t[g] * w_state_t[h:h + 1, :]).astype(BF16))
        lhs = jnp.concatenate([jnp.concatenate(m_blocks, axis=1), jnp.concatenate(bt_blocks, axis=1)], axis=0)
        xb = xs_ref[seq, :, j * pair:(j + 1) * pair]
        rhs = jnp.concatenate([jnp.where(low_half, xb, 0.0), jnp.where(low_half, 0.0, xb)], axis=0).astype(BF16)
        res = jnp.dot(lhs, rhs, preferred_element_type=F32)
        y_diag.append(res[:L])
        d_state.append(res[L:])

    state = state_ref[seq]
    y_off = [jnp.dot(c[:, g * n_state:(g + 1) * n_state],
                     state[:, g * group_w:(g + 1) * group_w].astype(BF16), preferred_element_type=F32)
             for g in range(SSM_GROUPS)]
    state_ref[seq] = state * e_out[L - 1:L, :] + jnp.concatenate(d_state, axis=1)

    y = jnp.concatenate(y_diag, axis=1) + jnp.concatenate(y_off, axis=1) * e_out + xs_ref[seq] * d_skip
    y = y * _silu(z)
    normed = []
    for g in range(SSM_GROUPS):
        yg = y[:, g * group_w:(g + 1) * group_w]
        normed.append(yg * lax.rsqrt(jnp.mean(yg * yg, axis=-1, keepdims=True) + EPS))
    return (jnp.concatenate(normed, axis=1) * norm_w).astype(BF16)


def _ssd_body(z_ref, xs_ref, b_ref, c_ref, dt_ref, dtb_ref, alog_ref, dskip_ref, nw_ref, e_ref, o_ref, state_ref):
    @pl.when(pl.program_id(1) == 0)
    def _():
        state_ref[...] = jnp.zeros_like(state_ref)

    for seq in range(SEQS_PER_STEP):
        o_ref[seq] = _ssd_chunk(z_ref[seq], xs_ref, b_ref[seq], c_ref[seq], dt_ref[seq], dtb_ref[...],
                                alog_ref[...], dskip_ref[...], nw_ref[...], e_ref[...], state_ref, seq)


def _ssd(z, xs, b, c, dt_raw, dt_bias, a_log, d_skip, norm_w, batch, seqlen):
    t, d_inner = xs.shape
    gn = b.shape[1]
    heads = d_inner // SSM_HEAD_DIM
    assert 3 * heads <= LANES and batch % SEQS_PER_STEP == 0
    nc = seqlen // CHUNK
    pad = lambda v: jnp.pad(v.reshape(1, heads), ((0, 0), (0, LANES - heads)))
    piece_head = jnp.where(jnp.arange(LANES) < 3 * heads, jnp.arange(LANES) % heads, -1)
    expand = (piece_head[:, None] == (jnp.arange(d_inner) // SSM_HEAD_DIM)[None, :]).astype(BF16)
    seq3 = lambda v: v.reshape(batch, seqlen, v.shape[1])
    blk = lambda n: pl.BlockSpec((SEQS_PER_STEP, CHUNK, n), lambda bi, ci: (bi, ci, 0))
    out = pl.pallas_call(
        _ssd_body,
        grid=(batch // SEQS_PER_STEP, nc),
        in_specs=[blk(d_inner), blk(d_inner), blk(gn), blk(gn), blk(LANES),
                  _resident((1, LANES)), _resident((1, LANES)), _resident((1, d_inner)),
                  _resident((1, d_inner)), _resident((LANES, d_inner))],
        out_specs=blk(d_inner),
        out_shape=jax.ShapeDtypeStruct((batch, seqlen, d_inner), BF16),
        scratch_shapes=[pltpu.VMEM((SEQS_PER_STEP, SSM_STATE, d_inner), F32)],
        compiler_params=_params(("arbitrary", "arbitrary")),
        name="ssd",
    )(seq3(z), seq3(xs), seq3(b), seq3(c), seq3(dt_raw), pad(dt_bias), pad(a_log),
      jnp.repeat(d_skip, SSM_HEAD_DIM).reshape(1, d_inner), norm_w.reshape(1, d_inner), expand)
    return out.reshape(t, d_inner)


def _attn_body(sink_ref, q_ref, kc_ref, kp_ref, vc_ref, vp_ref, o_ref):
    W = WINDOW
    n_blocks = q_ref.shape[1] // LANES
    blocks_per_kv = n_blocks // N_KV_HEADS
    row = lax.broadcasted_iota(jnp.int32, (W, 2 * W), 0)
    col = lax.broadcasted_iota(jnp.int32, (W, 2 * W), 1)
    in_window = (col <= row + W) & (col > row)

    def band(cur_ref, prev_ref, qi, kh):
        rows = slice(qi * W, (qi + 1) * W)
        parts = []
        for half in range(2):
            cs = slice((2 * kh + half) * LANES, (2 * kh + half + 1) * LANES)
            parts.append(prev_ref[:, cs] if qi == 0 else cur_ref[(qi - 1) * W:qi * W, cs])
            parts.append(cur_ref[rows, cs])
        return jnp.concatenate(parts, axis=0)

    scores = {}
    for qi in range(Q_BLOCKS_PER_STEP):
        rows = slice(qi * W, (qi + 1) * W)
        for kh in range(N_KV_HEADS):
            q_rows = jnp.concatenate([q_ref[rows, (kh * blocks_per_kv + jj) * LANES:(kh * blocks_per_kv + jj + 1) * LANES]
                                      for jj in range(blocks_per_kv)], axis=0)
            scores[qi, kh] = lax.dot_general(q_rows, band(kc_ref, kp_ref, qi, kh), (((1,), (1,)), ((), ())),
                                             preferred_element_type=F32)

    for qi in range(Q_BLOCKS_PER_STEP):
        rows = slice(qi * W, (qi + 1) * W)
        mask = in_window
        if qi == 0:
            mask = mask & (col >= jnp.where(pl.program_id(1) > 0, 0, W))
        for kh in range(N_KV_HEADS):
            s = scores[qi, kh]
            probs = []
            for jj in range(blocks_per_kv):
                halves = []
                for t in range(2):
                    sink = sink_ref[2 * (kh * blocks_per_kv + jj) + t]
                    sh = jnp.where(mask, s[jj * W:(jj + 1) * W, t * 2 * W:(t + 1) * 2 * W], -jnp.inf)
                    m = jnp.maximum(jnp.max(sh, axis=-1, keepdims=True), sink)
                    p = jnp.exp(sh - m)
                    denom = jnp.sum(p, axis=-1, keepdims=True) + jnp.exp(sink - m)
                    halves.append((p / denom).astype(BF16))
                probs.append(jnp.concatenate(halves, axis=1))
            o = jnp.dot(jnp.concatenate(probs, axis=0), band(vc_ref, vp_ref, qi, kh), preferred_element_type=F32)
            for jj in range(blocks_per_kv):
                j = kh * blocks_per_kv + jj
                o_ref[rows, j * LANES:(j + 1) * LANES] = o[jj * W:(jj + 1) * W].astype(BF16)


def _attn(sinks, q, k, v, batch, seqlen):
    t, nq = q.shape
    nkv = k.shape[1]
    rows = Q_BLOCKS_PER_STEP * WINDOW
    steps = seqlen // rows
    cur = lambda w: pl.BlockSpec((rows, w), lambda bi, ni: (bi * steps + ni, 0))
    prev = lambda w: pl.BlockSpec(
        (WINDOW, w), lambda bi, ni: ((bi * steps + ni) * Q_BLOCKS_PER_STEP - jnp.where(ni > 0, 1, 0), 0))
    return pl.pallas_call(
        _attn_body,
        grid=(batch, steps),
        in_specs=[pl.BlockSpec(memory_space=pltpu.SMEM), cur(nq), cur(nkv), prev(nkv), cur(nkv), prev(nkv)],
        out_specs=cur(nq),
        out_shape=jax.ShapeDtypeStruct((t, nq), BF16),
        compiler_params=_params(("parallel", "parallel")),
        name="attn",
    )(sinks, q, k, k, v, v)


def kernel(x, norm_w, ffn_w_gate, ffn_w_up, ffn_w_down, ssm_w_in, ssm_conv_w, ssm_conv_b, ssm_dt_bias, ssm_a_log, ssm_d, ssm_norm_w, ssm_w_out, kv_norm_w, w_k, b_k, w_v, b_v, attn_w_q, attn_b_q, attn_sinks, attn_w_o, attn_b_o, final_norm_w):
    batch, seqlen, d = x.shape
    assert seqlen % TOKEN_TILE == 0 and seqlen % CHUNK == 0 and seqlen % (Q_BLOCKS_PER_STEP * WINDOW) == 0
    assert ssm_w_in.shape[0] == 1 and attn_w_q.shape[0] == 1 and norm_w.shape[0] == 2
    d_inner = ssm_w_out.shape[1]
    heads = d_inner // SSM_HEAD_DIM
    conv_dim = ssm_conv_w.shape[-1]
    bf = lambda w: w.astype(BF16)
    xt = x.reshape(batch * seqlen, d)
    cos, sin = _rope_tables(seqlen)

    def block(v, layer, idx, **kw):
        return _block(v, norm_w[layer, 2 * idx], bf(ffn_w_gate[layer, idx]), bf(ffn_w_up[layer, idx]),
                      bf(ffn_w_down[layer, idx]), seqlen=seqlen, **kw)

    (xt,) = block(xt, 0, 0)
    w_in = ssm_w_in[0]
    w_dt = jnp.pad(w_in[:, d_inner + conv_dim:], ((0, 0), (0, LANES - heads)))
    z, xs, b, c, dt_raw = _ssm_in(xt, norm_w[0, 1], bf(w_in[:, :d_inner]), bf(w_in[:, d_inner:d_inner + conv_dim]),
                                  bf(w_dt), ssm_conv_w[0], ssm_conv_b[0], seqlen)
    yn = _ssd(z, xs, b, c, dt_raw, ssm_dt_bias[0], ssm_a_log[0], ssm_d[0], ssm_norm_w[0], batch, seqlen)
    xt, k, v = block(xt, 0, 1, pre=(yn, bf(ssm_w_out[0]), None),
                     post=("kv", kv_norm_w, bf(w_k), b_k, bf(w_v), b_v, cos, sin))

    xt, q = block(xt, 1, 0, post=("q", norm_w[1, 1], bf(attn_w_q[0]), attn_b_q[0], cos, sin))
    att = _attn(attn_sinks[0], q, k, v, batch, seqlen)
    (xt,) = block(xt, 1, 1, pre=(att, bf(attn_w_o[0]), attn_b_o[0]), final_nw=final_norm_w)
    return xt.reshape(batch, seqlen, d)
```

```python
import functools
import math

import jax
import jax.numpy as jnp
from jax import lax
from jax.experimental import pallas as pl
from jax.experimental.pallas import tpu as pltpu

F32 = jnp.float32
BF16 = jnp.bfloat16

SSM_HEAD_DIM = 64
SSM_GROUPS = 4
SSM_STATE = 128
CONV_WIDTH = 4
CHUNK = 128
ATT_HEAD_DIM = 64
N_KV_HEADS = 4
WINDOW = 128
ROPE_THETA = 10000.0
FFN_RES_WEIGHT = 0.5
EPS = 1e-5
LOG2_E = 1.0 / math.log(2.0)

LANES = 128
SUBLANES = 8
VMEM_LIMIT_BYTES = 56 * 1024 * 1024

TOKEN_TILE = 512
CONV_COL_BLOCK = 512
CONV_ROW_STRIP = 32
SEQS_PER_STEP = 2
Q_BLOCKS_PER_STEP = 2


def _rmsnorm(x, w):
    return x * lax.rsqrt(jnp.mean(x * x, axis=-1, keepdims=True) + EPS) * w


def _silu(x):
    return x / (1.0 + jnp.exp(-x))


def _softplus(x):
    return jnp.maximum(x, 0.0) + jnp.log(1.0 + jnp.exp(-jnp.abs(x)))


def _resident(shape):
    zeros = (0,) * len(shape)
    return pl.BlockSpec(shape, lambda *_: zeros, pipeline_mode=pl.Buffered(1))


def _params(semantics):
    return pltpu.CompilerParams(dimension_semantics=semantics, vmem_limit_bytes=VMEM_LIMIT_BYTES)


def _rope_tables(seqlen):
    pos = jnp.arange(seqlen, dtype=F32)
    inv = 1.0 / (ROPE_THETA ** (jnp.arange(0, ATT_HEAD_DIM, 2, dtype=F32) / ATT_HEAD_DIM))
    ang = pos[:, None] * inv[None, :]
    cos, sin = jnp.cos(ang), jnp.sin(ang)
    reps = LANES // ATT_HEAD_DIM
    return (jnp.tile(jnp.concatenate([cos, cos], axis=-1), (1, reps)),
            jnp.tile(jnp.concatenate([-sin, sin], axis=-1), (1, reps)))


def _lane_blocks(t):
    return [t[:, j * LANES:(j + 1) * LANES] for j in range(t.shape[1] // LANES)]


def _rope(t, cos, sin_signed):
    half = ATT_HEAD_DIM // 2
    lane = lax.broadcasted_iota(jnp.int32, (t.shape[0], LANES), 1)
    first_half = (lane % ATT_HEAD_DIM) < half
    out = []
    for blk in _lane_blocks(t):
        partner = jnp.where(first_half, pltpu.roll(blk, LANES - half, axis=1), pltpu.roll(blk, half, axis=1))
        out.append(blk * cos + partner * sin_signed)
    return jnp.concatenate(out, axis=1)


def _separate_heads(t):
    low_half = lax.broadcasted_iota(jnp.int32, (t.shape[0], LANES), 1) < ATT_HEAD_DIM
    out = []
    for blk in _lane_blocks(t):
        swapped = pltpu.roll(blk, ATT_HEAD_DIM, axis=1)
        out += [jnp.where(low_half, blk, 0.0), jnp.where(low_half, 0.0, swapped),
                jnp.where(low_half, swapped, 0.0), jnp.where(low_half, 0.0, blk)]
    return jnp.concatenate(out, axis=1)


def _block_body(*refs, has_pre, has_pre_bias, post, final_norm, scale):
    it = iter(refs)
    x = next(it)[...]
    if has_pre:
        a_ref, wp_ref = next(it), next(it)
        x = x + jnp.dot(a_ref[...], wp_ref[...], preferred_element_type=F32)
        if has_pre_bias:
            x = x + next(it)[...]
    nw_ref, wg_ref, wu_ref, wd_ref = next(it), next(it), next(it), next(it)
    h = _rmsnorm(x, nw_ref[...]).astype(BF16)
    g = jnp.dot(h, wg_ref[...], preferred_element_type=F32)
    u = jnp.dot(h, wu_ref[...], preferred_element_type=F32)
    a = (_silu(g) * u).astype(BF16)
    y = x + FFN_RES_WEIGHT * jnp.dot(a, wd_ref[...], preferred_element_type=F32)
    if post == "kv":
        pnw_ref, wk_ref, bk_ref, wv_ref, bv_ref, cos_ref, sin_ref = (next(it) for _ in range(7))
        hp = _rmsnorm(y, pnw_ref[...]).astype(BF16)
        k = jnp.dot(hp, wk_ref[...], preferred_element_type=F32) + bk_ref[...]
        v = jnp.dot(hp, wv_ref[...], preferred_element_type=F32) + bv_ref[...]
        k = _rope(k, cos_ref[...], sin_ref[...])
    elif post == "q":
        pnw_ref, wq_ref, bq_ref, cos_ref, sin_ref = (next(it) for _ in range(5))
        hp = _rmsnorm(y, pnw_ref[...]).astype(BF16)
        q = jnp.dot(hp, wq_ref[...], preferred_element_type=F32) + bq_ref[...]
        q = _rope(q, cos_ref[...], sin_ref[...]) * scale
    if final_norm:
        y = _rmsnorm(y, next(it)[...])
    next(it)[...] = y
    if post == "kv":
        next(it)[...] = _separate_heads(k).astype(BF16)
        next(it)[...] = _separate_heads(v).astype(BF16)
    elif post == "q":
        next(it)[...] = q.astype(BF16)


def _block(x, nw, ffn_weights, ffn_index, *, pre=None, post=None, final_nw=None, seqlen=None):
    t, d = x.shape
    tm = TOKEN_TILE
    row = lambda w: pl.BlockSpec((tm, w), lambda i: (i, 0))
    vec = lambda v: v.reshape(1, -1)
    args, in_specs = [x], [row(d)]

    def add_resident(*arrays):
        for arr in arrays:
            args.append(arr)
            in_specs.append(_resident(arr.shape))

    if pre is not None:
        a, wp, bp = pre
        args.append(a)
        in_specs.append(row(a.shape[1]))
        add_resident(wp)
        if bp is not None:
            add_resident(vec(bp))
    add_resident(vec(nw))
    n_lead = len(ffn_index)
    for w in ffn_weights:
        args.append(w)
        in_specs.append(pl.BlockSpec((None,) * n_lead + w.shape[n_lead:],
                                     lambda *_: tuple(ffn_index) + (0,) * (w.ndim - n_lead),
                                     pipeline_mode=pl.Buffered(1)))
    out_specs = [row(d)]
    out_shape = [jax.ShapeDtypeStruct((t, d), F32)]
    kind = None
    if post is not None:
        kind = post[0]
        table = pl.BlockSpec((tm, LANES), lambda i: (i % (seqlen // tm), 0))
        if kind == "kv":
            _, pnw, wk, bk, wv, bv, cos, sin = post
            add_resident(vec(pnw), wk, vec(bk), wv, vec(bv))
            sep = 2 * (LANES // ATT_HEAD_DIM) * wk.shape[1]
            out_specs += [row(sep), row(sep)]
            out_shape += [jax.ShapeDtypeStruct((t, sep), BF16)] * 2
        else:
            _, pnw, wq, bq, cos, sin = post
            add_resident(vec(pnw), wq, vec(bq))
            out_specs.append(row(wq.shape[1]))
            out_shape.append(jax.ShapeDtypeStruct((t, wq.shape[1]), BF16))
        args += [cos, sin]
        in_specs += [table, table]
    if final_nw is not None:
        add_resident(vec(final_nw))
    body = functools.partial(_block_body, has_pre=pre is not None, has_pre_bias=pre is not None and pre[2] is not None,
                             post=kind, final_norm=final_nw is not None, scale=1.0 / math.sqrt(ATT_HEAD_DIM))
    return pl.pallas_call(
        body,
        grid=(t // tm,),
        in_specs=in_specs,
        out_specs=out_specs,
        out_shape=out_shape,
        compiler_params=_params(("parallel",)),
        name="block",
    )(*args)


def _ssm_in_body(x_ref, nw_ref, w_ref, wdt_ref, cw_ref, cb_ref,
                 gate_ref, xs_ref, b_ref, c_ref, dt_ref, u_ref, *, tiles_per_seq, d_inner, gn):
    tm = x_ref.shape[0]
    conv_dim = d_inner + 2 * gn
    cblk = CONV_COL_BLOCK
    n_blk = conv_dim // cblk

    @pl.when(pl.program_id(0) % tiles_per_seq == 0)
    def _():
        u_ref[:, :SUBLANES, :] = jnp.zeros((u_ref.shape[0], SUBLANES, LANES), F32)

    h = _rmsnorm(x_ref[...], nw_ref[...]).astype(BF16)

    def project(blk):
        return jnp.dot(h, w_ref[:, d_inner + blk * cblk:d_inner + (blk + 1) * cblk], preferred_element_type=F32)

    def conv(u, blk):
        lo = blk * cblk
        for s in range(cblk // LANES):
            slab = blk * (cblk // LANES) + s
            cs = slice(lo + s * LANES, lo + (s + 1) * LANES)
            u_ref[slab, SUBLANES:, :] = u[:, s * LANES:(s + 1) * LANES]
            acc = cb_ref[:, cs]
            for k in range(CONV_WIDTH):
                shift = CONV_WIDTH - 1 - k
                acc = acc + cw_ref[k:k + 1, cs] * u_ref[slab, pl.ds(SUBLANES - shift, tm), :]
            act = _silu(acc)
            if lo < d_inner:
                xs_ref[:, cs] = act
            elif lo < d_inner + gn:
                b_ref[:, lo - d_inner + s * LANES:lo - d_inner + (s + 1) * LANES] = act.astype(BF16)
            else:
                off = lo - d_inner - gn
                c_ref[:, off + s * LANES:off + (s + 1) * LANES] = act.astype(BF16)
            u_ref[slab, :SUBLANES, :] = u_ref[slab, tm:, :]

    u_next = project(0)
    for blk in range(n_blk):
        u = u_next
        if blk + 1 < n_blk:
            u_next = project(blk + 1)
        else:
            gate_ref[...] = _silu(jnp.dot(h, w_ref[:, :d_inner], preferred_element_type=F32))
            dt_ref[...] = jnp.dot(h, wdt_ref[...], preferred_element_type=F32)
        conv(u, blk)


def _ssm_in(x, nw, w_in, wdt, conv_w, conv_b, d_inner, seqlen):
    t, d = x.shape
    conv_dim = conv_w.shape[1]
    gn = (conv_dim - d_inner) // 2
    assert d_inner % CONV_COL_BLOCK == 0 and gn % CONV_COL_BLOCK == 0
    tm = TOKEN_TILE
    row = lambda n: pl.BlockSpec((tm, n), lambda i: (i, 0))
    return pl.pallas_call(
        functools.partial(_ssm_in_body, tiles_per_seq=seqlen // tm, d_inner=d_inner, gn=gn),
        grid=(t // tm,),
        in_specs=[row(d), _resident((1, d)), _resident(w_in.shape),
                  _resident((d, LANES)), _resident((CONV_WIDTH, conv_dim)), _resident((1, conv_dim))],
        out_specs=[row(d_inner), row(d_inner), row(gn), row(gn), row(LANES)],
        out_shape=[jax.ShapeDtypeStruct((t, d_inner), F32), jax.ShapeDtypeStruct((t, d_inner), F32),
                   jax.ShapeDtypeStruct((t, gn), BF16), jax.ShapeDtypeStruct((t, gn), BF16),
                   jax.ShapeDtypeStruct((t, LANES), F32)],
        scratch_shapes=[pltpu.VMEM((conv_dim // LANES, SUBLANES + tm, LANES), F32)],
        compiler_params=_params(("arbitrary",)),
        name="ssm_in",
    )(x, nw.reshape(1, d), w_in, wdt, conv_w, conv_b.reshape(1, conv_dim))


def _cumsum_rows(x):
    n = x.shape[0]
    row = lax.broadcasted_iota(jnp.int32, x.shape, 0)
    step = 1
    while step < n:
        x = x + jnp.where(row >= step, pltpu.roll(x, step, axis=0), 0.0)
        step *= 2
    return x


def _expand_heads(v, e, heads):
    valid = lax.broadcasted_iota(jnp.int32, v.shape, 1) < heads
    hi = v.astype(BF16).astype(F32)
    r1 = v - hi
    mid = r1.astype(BF16).astype(F32)
    lo = (r1 - mid).astype(BF16).astype(F32)
    keep = lambda p: jnp.where(valid, p, 0.0)
    packed = keep(hi) + pltpu.roll(keep(mid), heads, axis=1) + pltpu.roll(keep(lo), 2 * heads, axis=1)
    return jnp.dot(packed.astype(BF16), e, preferred_element_type=F32)


def _ssd_body(gate_ref, xs_ref, b_ref, c_ref, dt_ref, dtb_ref, alog_ref, dskip_ref, nw_ref, e_ref, o_ref, state_ref):
    L = CHUNK
    n_state = SSM_STATE
    pair = 2 * SSM_HEAD_DIM
    d_inner = xs_ref.shape[-1]
    heads = d_inner // SSM_HEAD_DIM
    n_pairs = d_inner // pair
    pairs_per_group = n_pairs // SSM_GROUPS
    group_w = d_inner // SSM_GROUPS
    seqs = range(SEQS_PER_STEP)

    @pl.when(pl.program_id(1) == 0)
    def _():
        state_ref[...] = jnp.zeros_like(state_ref)

    row = lax.broadcasted_iota(jnp.int32, (L, L), 0)
    col = lax.broadcasted_iota(jnp.int32, (L, L), 1)
    causal = col <= row
    low_half = lax.broadcasted_iota(jnp.int32, (L, pair), 1) < SSM_HEAD_DIM
    group = lambda ref, seq, g: ref[seq, :, g * n_state:(g + 1) * n_state]

    def prepare(seq):
        dt = _softplus(dt_ref[seq] + dtb_ref[...])
        acs = _cumsum_rows(dt * (-jnp.exp(alog_ref[...]) * LOG2_E))
        acs_t = acs.T
        dt_t = dt.T
        cb, b_t = [], []
        for g in range(SSM_GROUPS):
            bg = group(b_ref, seq, g)
            cb.append(lax.dot_general(group(c_ref, seq, g), bg, (((1,), (1,)), ((), ())),
                                      preferred_element_type=F32))
            b_t.append(bg.astype(F32).T)
        return dict(
            acs=acs, cb=cb, b_t=b_t,
            e_out=_expand_heads(jnp.exp2(acs), e_ref[...], heads),
            w_state_t=jnp.exp2(acs_t[:, L - 1:L] - acs_t) * dt_t,
            src_t=acs_t - jnp.log2(dt_t))

    def head_pair(seq, p, j):
        g = j // pairs_per_group
        m_blocks, bt_blocks = [], []
        for h in (2 * j, 2 * j + 1):
            seg = p["acs"][:, h:h + 1] - p["src_t"][h:h + 1, :]
            m_blocks.append((p["cb"][g] * jnp.exp2(jnp.where(causal, seg, -jnp.inf))).astype(BF16))
            bt_blocks.append((p["b_t"][g] * p["w_state_t"][h:h + 1, :]).astype(BF16))
        lhs = jnp.concatenate([jnp.concatenate(m_blocks, axis=1), jnp.concatenate(bt_blocks, axis=1)], axis=0)
        xb = xs_ref[seq, :, j * pair:(j + 1) * pair]
        rhs = jnp.concatenate([jnp.where(low_half, xb, 0.0), jnp.where(low_half, 0.0, xb)], axis=0).astype(BF16)
        res = jnp.dot(lhs, rhs, preferred_element_type=F32)
        return res[:L], res[L:]

    def finish(seq, p, y_diag, d_state):
        state = state_ref[seq]
        y_off = [jnp.dot(group(c_ref, seq, g), state[:, g * group_w:(g + 1) * group_w].astype(BF16),
                         preferred_element_type=F32) for g in range(SSM_GROUPS)]
        state_ref[seq] = state * p["e_out"][L - 1:L, :] + jnp.concatenate(d_state, axis=1)
        y = (jnp.concatenate(y_diag, axis=1) + jnp.concatenate(y_off, axis=1) * p["e_out"]
             + xs_ref[seq] * dskip_ref[...])
        y = y * gate_ref[seq]
        normed = []
        for g in range(SSM_GROUPS):
            yg = y[:, g * group_w:(g + 1) * group_w]
            normed.append(yg * lax.rsqrt(jnp.mean(yg * yg, axis=-1, keepdims=True) + EPS))
        o_ref[seq] = (jnp.concatenate(normed, axis=1) * nw_ref[...]).astype(BF16)

    prepared = [prepare(seq) for seq in seqs]
    results = [([], []) for _ in seqs]
    for j in range(n_pairs):
        for seq in seqs:
            y_j, s_j = head_pair(seq, prepared[seq], j)
            results[seq][0].append(y_j)
            results[seq][1].append(s_j)
    for seq in seqs:
        finish(seq, prepared[seq], *results[seq])


def _ssd(gate, xs, b, c, dt_raw, dt_bias, a_log, d_skip, norm_w, batch, seqlen):
    t, d_inner = xs.shape
    gn = b.shape[1]
    heads = d_inner // SSM_HEAD_DIM
    assert 3 * heads <= LANES and batch % SEQS_PER_STEP == 0
    nc = seqlen // CHUNK
    pad = lambda v: jnp.pad(v.reshape(1, heads), ((0, 0), (0, LANES - heads)))
    piece_head = jnp.where(jnp.arange(LANES) < 3 * heads, jnp.arange(LANES) % heads, -1)
    expand = (piece_head[:, None] == (jnp.arange(d_inner) // SSM_HEAD_DIM)[None, :]).astype(BF16)
    seq3 = lambda v: v.reshape(batch, seqlen, v.shape[1])
    blk = lambda n: pl.BlockSpec((SEQS_PER_STEP, CHUNK, n), lambda bi, ci: (bi, ci, 0))
    out = pl.pallas_call(
        _ssd_body,
        grid=(batch // SEQS_PER_STEP, nc),
        in_specs=[blk(d_inner), blk(d_inner), blk(gn), blk(gn), blk(LANES),
                  _resident((1, LANES)), _resident((1, LANES)), _resident((1, d_inner)),
                  _resident((1, d_inner)), _resident((LANES, d_inner))],
        out_specs=blk(d_inner),
        out_shape=jax.ShapeDtypeStruct((batch, seqlen, d_inner), BF16),
        scratch_shapes=[pltpu.VMEM((SEQS_PER_STEP, SSM_STATE, d_inner), F32)],
        compiler_params=_params(("arbitrary", "arbitrary")),
        name="ssd",
    )(seq3(gate), seq3(xs), seq3(b), seq3(c), seq3(dt_raw), pad(dt_bias), pad(a_log),
      jnp.repeat(d_skip, SSM_HEAD_DIM).reshape(1, d_inner), norm_w.reshape(1, d_inner), expand)
    return out.reshape(t, d_inner)


def _attn_body(sink_ref, q_ref, kc_ref, kp_ref, vc_ref, vp_ref, o_ref):
    W = WINDOW
    n_blocks = q_ref.shape[1] // LANES
    blocks_per_kv = n_blocks // N_KV_HEADS
    row = lax.broadcasted_iota(jnp.int32, (W, 2 * W), 0)
    col = lax.broadcasted_iota(jnp.int32, (W, 2 * W), 1)
    in_window = (col <= row + W) & (col > row)

    def band(cur_ref, prev_ref, qi, kh):
        rows = slice(qi * W, (qi + 1) * W)
        parts = []
        for half in range(2):
            cs = slice((2 * kh + half) * LANES, (2 * kh + half + 1) * LANES)
            parts.append(prev_ref[:, cs] if qi == 0 else cur_ref[(qi - 1) * W:qi * W, cs])
            parts.append(cur_ref[rows, cs])
        return jnp.concatenate(parts, axis=0)

    scores = {}
    for qi in range(Q_BLOCKS_PER_STEP):
        rows = slice(qi * W, (qi + 1) * W)
        for kh in range(N_KV_HEADS):
            q_rows = jnp.concatenate([q_ref[rows, (kh * blocks_per_kv + jj) * LANES:(kh * blocks_per_kv + jj + 1) * LANES]
                                      for jj in range(blocks_per_kv)], axis=0)
            scores[qi, kh] = lax.dot_general(q_rows, band(kc_ref, kp_ref, qi, kh), (((1,), (1,)), ((), ())),
                                             preferred_element_type=F32)

    for qi in range(Q_BLOCKS_PER_STEP):
        rows = slice(qi * W, (qi + 1) * W)
        mask = in_window
        if qi == 0:
            mask = mask & (col >= jnp.where(pl.program_id(1) > 0, 0, W))
        for kh in range(N_KV_HEADS):
            s = scores[qi, kh]
            probs = []
            for jj in range(blocks_per_kv):
                halves = []
                for t in range(2):
                    sink = sink_ref[2 * (kh * blocks_per_kv + jj) + t]
                    sh = jnp.where(mask, s[jj * W:(jj + 1) * W, t * 2 * W:(t + 1) * 2 * W], -jnp.inf)
                    m = jnp.maximum(jnp.max(sh, axis=-1, keepdims=True), sink)
                    p = jnp.exp(sh - m)
                    denom = jnp.sum(p, axis=-1, keepdims=True) + jnp.exp(sink - m)
                    halves.append((p / denom).astype(BF16))
                probs.append(jnp.concatenate(halves, axis=1))
            o = jnp.dot(jnp.concatenate(probs, axis=0), band(vc_ref, vp_ref, qi, kh), preferred_element_type=F32)
            for jj in range(blocks_per_kv):
                j = kh * blocks_per_kv + jj
                o_ref[rows, j * LANES:(j + 1) * LANES] = o[jj * W:(jj + 1) * W].astype(BF16)


def _attn(sinks, q, k, v, batch, seqlen):
    t, nq = q.shape
    nkv = k.shape[1]
    rows = Q_BLOCKS_PER_STEP * WINDOW
    steps = seqlen // rows
    cur = lambda w: pl.BlockSpec((rows, w), lambda bi, ni: (bi * steps + ni, 0))
    prev = lambda w: pl.BlockSpec(
        (WINDOW, w), lambda bi, ni: ((bi * steps + ni) * Q_BLOCKS_PER_STEP - jnp.where(ni > 0, 1, 0), 0))
    return pl.pallas_call(
        _attn_body,
        grid=(batch, steps),
        in_specs=[pl.BlockSpec(memory_space=pltpu.SMEM), cur(nq), cur(nkv), prev(nkv), cur(nkv), prev(nkv)],
        out_specs=cur(nq),
        out_shape=jax.ShapeDtypeStruct((t, nq), BF16),
        compiler_params=_params(("parallel", "parallel")),
        name="attn",
    )(sinks, q, k, k, v, v)


def kernel(x, norm_w, ffn_w_gate, ffn_w_up, ffn_w_down, ssm_w_in, ssm_conv_w, ssm_conv_b, ssm_dt_bias, ssm_a_log, ssm_d, ssm_norm_w, ssm_w_out, kv_norm_w, w_k, b_k, w_v, b_v, attn_w_q, attn_b_q, attn_sinks, attn_w_o, attn_b_o, final_norm_w):
    batch, seqlen, d = x.shape
    assert seqlen % TOKEN_TILE == 0 and seqlen % CHUNK == 0 and seqlen % (Q_BLOCKS_PER_STEP * WINDOW) == 0
    assert ssm_w_in.shape[0] == 1 and attn_w_q.shape[0] == 1 and norm_w.shape[0] == 2
    d_inner = ssm_w_out.shape[1]
    heads = d_inner // SSM_HEAD_DIM
    conv_dim = ssm_conv_w.shape[-1]
    bf = lambda w: w.astype(BF16)
    xt = x.reshape(batch * seqlen, d)
    cos, sin = _rope_tables(seqlen)

    ffn_weights = (bf(ffn_w_gate), bf(ffn_w_up), bf(ffn_w_down))

    def block(v, layer, idx, **kw):
        return _block(v, norm_w[layer, 2 * idx], ffn_weights, (layer, idx), seqlen=seqlen, **kw)

    (xt,) = block(xt, 0, 0)
    w_in = bf(ssm_w_in[0])
    w_dt = jnp.pad(w_in[:, d_inner + conv_dim:], ((0, 0), (0, LANES - heads)))
    gate, xs, b, c, dt_raw = _ssm_in(xt, norm_w[0, 1], w_in, w_dt, ssm_conv_w[0], ssm_conv_b[0], d_inner, seqlen)
    yn = _ssd(gate, xs, b, c, dt_raw, ssm_dt_bias[0], ssm_a_log[0], ssm_d[0], ssm_norm_w[0], batch, seqlen)
    xt, k, v = block(xt, 0, 1, pre=(yn, bf(ssm_w_out[0]), None),
                     post=("kv", kv_norm_w, bf(w_k), b_k, bf(w_v), b_v, cos, sin))

    xt, q = block(xt, 1, 0, post=("q", norm_w[1, 1], bf(attn_w_q[0]), attn_b_q[0], cos, sin))
    att = _attn(attn_sinks[0], q, k, v, batch, seqlen)
    (xt,) = block(xt, 1, 1, pre=(att, bf(attn_w_o[0]), attn_b_o[0]), final_nw=final_norm_w)
    return xt.reshape(batch, seqlen, d)
```

```python
import functools
import math

import jax
import jax.numpy as jnp
from jax import lax
from jax.experimental import pallas as pl
from jax.experimental.pallas import tpu as pltpu

F32 = jnp.float32
BF16 = jnp.bfloat16

SSM_HEAD_DIM = 64
SSM_GROUPS = 4
SSM_STATE = 128
CONV_WIDTH = 4
CHUNK = 128
ATT_HEAD_DIM = 64
N_KV_HEADS = 4
WINDOW = 128
ROPE_THETA = 10000.0
FFN_RES_WEIGHT = 0.5
EPS = 1e-5
LOG2_E = 1.0 / math.log(2.0)

LANES = 128
SUBLANES = 8
VMEM_LIMIT_BYTES = 56 * 1024 * 1024

TOKEN_TILE = 512
BLOCK_SUBTILES = 2
CONV_COL_BLOCK = 512
CONV_ROW_STRIP = 32
SEQS_PER_STEP = 2
Q_BLOCKS_PER_STEP = 2


def _rmsnorm(x, w):
    return x * lax.rsqrt(jnp.mean(x * x, axis=-1, keepdims=True) + EPS) * w


def _silu(x):
    return x / (1.0 + jnp.exp(-x))


def _softplus(x):
    return jnp.maximum(x, 0.0) + jnp.log(1.0 + jnp.exp(-jnp.abs(x)))


def _resident(shape):
    zeros = (0,) * len(shape)
    return pl.BlockSpec(shape, lambda *_: zeros, pipeline_mode=pl.Buffered(1))


def _params(semantics):
    return pltpu.CompilerParams(dimension_semantics=semantics, vmem_limit_bytes=VMEM_LIMIT_BYTES)


def _rope_tables(seqlen):
    pos = jnp.arange(seqlen, dtype=F32)
    inv = 1.0 / (ROPE_THETA ** (jnp.arange(0, ATT_HEAD_DIM, 2, dtype=F32) / ATT_HEAD_DIM))
    ang = pos[:, None] * inv[None, :]
    cos, sin = jnp.cos(ang), jnp.sin(ang)
    reps = LANES // ATT_HEAD_DIM
    return (jnp.tile(jnp.concatenate([cos, cos], axis=-1), (1, reps)),
            jnp.tile(jnp.concatenate([-sin, sin], axis=-1), (1, reps)))


def _lane_blocks(t):
    return [t[:, j * LANES:(j + 1) * LANES] for j in range(t.shape[1] // LANES)]


def _rope(t, cos, sin_signed):
    half = ATT_HEAD_DIM // 2
    lane = lax.broadcasted_iota(jnp.int32, (t.shape[0], LANES), 1)
    first_half = (lane % ATT_HEAD_DIM) < half
    out = []
    for blk in _lane_blocks(t):
        partner = jnp.where(first_half, pltpu.roll(blk, LANES - half, axis=1), pltpu.roll(blk, half, axis=1))
        out.append(blk * cos + partner * sin_signed)
    return jnp.concatenate(out, axis=1)


def _separate_heads(t):
    low_half = lax.broadcasted_iota(jnp.int32, (t.shape[0], LANES), 1) < ATT_HEAD_DIM
    out = []
    for blk in _lane_blocks(t):
        swapped = pltpu.roll(blk, ATT_HEAD_DIM, axis=1)
        out += [jnp.where(low_half, blk, 0.0), jnp.where(low_half, 0.0, swapped),
                jnp.where(low_half, swapped, 0.0), jnp.where(low_half, 0.0, blk)]
    return jnp.concatenate(out, axis=1)


def _block_body(*refs, has_pre, has_pre_bias, post, final_norm, scale, n_cast):
    it = iter(refs)
    x_ref = next(it)
    a_ref, wp_ref, bp_ref = (next(it), next(it), next(it) if has_pre_bias else None) if has_pre else (None,) * 3
    nw_ref, wg_ref, wu_ref, wd_ref = next(it), next(it), next(it), next(it)
    if post == "kv":
        pnw_ref, wk_ref, bk_ref, wv_ref, bv_ref, cos_ref, sin_ref = (next(it) for _ in range(7))
    elif post == "q":
        pnw_ref, wq_ref, bq_ref, cos_ref, sin_ref = (next(it) for _ in range(5))
    fnw_ref = next(it) if final_norm else None
    cast_in = [next(it) for _ in range(n_cast)]
    o_ref = next(it)
    outs = list(it)
    post_out, cast_out = outs[:len(outs) - n_cast], outs[len(outs) - n_cast:]
    for src, dst in zip(cast_in, cast_out):
        dst[...] = src[...].astype(BF16)

    sub = x_ref.shape[0] // BLOCK_SUBTILES
    dot = lambda a, w_ref: jnp.dot(a, w_ref[...], preferred_element_type=F32)

    def load(s, rows):
        s["x"] = x_ref[rows]
        if has_pre:
            s["x"] = s["x"] + dot(a_ref[rows], wp_ref)
            if has_pre_bias:
                s["x"] = s["x"] + bp_ref[...]

    def norm(s, rows):
        s["h"] = _rmsnorm(s["x"], nw_ref[...]).astype(BF16)

    def gate_up(s, rows):
        s["g"] = dot(s["h"], wg_ref)
        s["u"] = dot(s["h"], wu_ref)

    def activate(s, rows):
        s["a"] = (_silu(s["g"]) * s["u"]).astype(BF16)

    def down(s, rows):
        s["y"] = s["x"] + FFN_RES_WEIGHT * dot(s["a"], wd_ref)

    def post_norm(s, rows):
        s["hp"] = _rmsnorm(s["y"], pnw_ref[...]).astype(BF16)

    def post_project(s, rows):
        if post == "kv":
            s["k"] = dot(s["hp"], wk_ref) + bk_ref[...]
            s["v"] = dot(s["hp"], wv_ref) + bv_ref[...]
        else:
            s["q"] = dot(s["hp"], wq_ref) + bq_ref[...]

    def store(s, rows):
        o_ref[rows] = _rmsnorm(s["y"], fnw_ref[...]) if final_norm else s["y"]
        if post == "kv":
            post_out[0][rows] = _separate_heads(_rope(s["k"], cos_ref[rows], sin_ref[rows])).astype(BF16)
            post_out[1][rows] = _separate_heads(s["v"]).astype(BF16)
        elif post == "q":
            post_out[0][rows] = (_rope(s["q"], cos_ref[rows], sin_ref[rows]) * scale).astype(BF16)

    stages = [load, norm, gate_up, activate, down] + ([post_norm, post_project] if post else []) + [store]
    states = [{} for _ in range(BLOCK_SUBTILES)]
    for stage in stages:
        for i, s in enumerate(states):
            stage(s, slice(i * sub, (i + 1) * sub))


def _block(x, nw, ffn_weights, ffn_index, *, pre=None, post=None, final_nw=None, seqlen=None, cast=()):
    t, d = x.shape
    tm = TOKEN_TILE
    row = lambda w: pl.BlockSpec((tm, w), lambda i: (i, 0))
    vec = lambda v: v.reshape(1, -1)
    args, in_specs = [x], [row(d)]

    def add_resident(*arrays):
        for arr in arrays:
            args.append(arr)
            in_specs.append(_resident(arr.shape))

    if pre is not None:
        a, wp, bp = pre
        args.append(a)
        in_specs.append(row(a.shape[1]))
        add_resident(wp)
        if bp is not None:
            add_resident(vec(bp))
    add_resident(vec(nw))
    n_lead = len(ffn_index)
    for w in ffn_weights:
        args.append(w)
        in_specs.append(pl.BlockSpec((None,) * n_lead + w.shape[n_lead:],
                                     lambda *_: tuple(ffn_index) + (0,) * (w.ndim - n_lead),
                                     pipeline_mode=pl.Buffered(1)))
    out_specs = [row(d)]
    out_shape = [jax.ShapeDtypeStruct((t, d), F32)]
    kind = None
    if post is not None:
        kind = post[0]
        table = pl.BlockSpec((tm, LANES), lambda i: (i % (seqlen // tm), 0))
        if kind == "kv":
            _, pnw, wk, bk, wv, bv, cos, sin = post
            add_resident(vec(pnw), wk, vec(bk), wv, vec(bv))
            sep = 2 * (LANES // ATT_HEAD_DIM) * wk.shape[1]
            out_specs += [row(sep), row(sep)]
            out_shape += [jax.ShapeDtypeStruct((t, sep), BF16)] * 2
        else:
            _, pnw, wq, bq, cos, sin = post
            add_resident(vec(pnw), wq, vec(bq))
            out_specs.append(row(wq.shape[1]))
            out_shape.append(jax.ShapeDtypeStruct((t, wq.shape[1]), BF16))
        args += [cos, sin]
        in_specs += [table, table]
    if final_nw is not None:
        add_resident(vec(final_nw))
    cast_specs, cast_shapes = _cast_specs(cast, t // tm, lambda i: i)
    args += list(cast)
    in_specs += cast_specs
    out_specs += cast_specs
    out_shape += cast_shapes
    body = functools.partial(_block_body, has_pre=pre is not None, has_pre_bias=pre is not None and pre[2] is not None,
                             post=kind, final_norm=final_nw is not None, scale=1.0 / math.sqrt(ATT_HEAD_DIM),
                             n_cast=len(cast))
    return pl.pallas_call(
        body,
        grid=(t // tm,),
        in_specs=in_specs,
        out_specs=out_specs,
        out_shape=out_shape,
        compiler_params=_params(("parallel",)),
        name="block",
    )(*args)


def _ssm_in_body(x_ref, nw_ref, w_ref, wdt_ref, cw_ref, cb_ref,
                 gate_ref, xs_ref, b_ref, c_ref, dt_ref, u_ref, *, tiles_per_seq, d_inner, gn):
    tm = x_ref.shape[0]
    conv_dim = d_inner + 2 * gn
    cblk = CONV_COL_BLOCK
    n_blk = conv_dim // cblk

    @pl.when(pl.program_id(0) % tiles_per_seq == 0)
    def _():
        u_ref[:, :SUBLANES, :] = jnp.zeros((u_ref.shape[0], SUBLANES, LANES), F32)

    h = _rmsnorm(x_ref[...], nw_ref[...]).astype(BF16)

    def project(blk):
        return jnp.dot(h, w_ref[:, d_inner + blk * cblk:d_inner + (blk + 1) * cblk], preferred_element_type=F32)

    def conv(u, blk):
        lo = blk * cblk
        for s in range(cblk // LANES):
            slab = blk * (cblk // LANES) + s
            cs = slice(lo + s * LANES, lo + (s + 1) * LANES)
            u_ref[slab, SUBLANES:, :] = u[:, s * LANES:(s + 1) * LANES]
            acc = cb_ref[:, cs]
            for k in range(CONV_WIDTH):
                shift = CONV_WIDTH - 1 - k
                acc = acc + cw_ref[k:k + 1, cs] * u_ref[slab, pl.ds(SUBLANES - shift, tm), :]
            act = _silu(acc)
            if lo < d_inner:
                xs_ref[:, cs] = act
            elif lo < d_inner + gn:
                b_ref[:, lo - d_inner + s * LANES:lo - d_inner + (s + 1) * LANES] = act.astype(BF16)
            else:
                off = lo - d_inner - gn
                c_ref[:, off + s * LANES:off + (s + 1) * LANES] = act.astype(BF16)
            u_ref[slab, :SUBLANES, :] = u_ref[slab, tm:, :]

    u_next = project(0)
    for blk in range(n_blk):
        u = u_next
        if blk + 1 < n_blk:
            u_next = project(blk + 1)
        else:
            gate_ref[...] = _silu(jnp.dot(h, w_ref[:, :d_inner], preferred_element_type=F32))
            dt_ref[...] = jnp.dot(h, wdt_ref[...], preferred_element_type=F32)
        conv(u, blk)


def _ssm_in(x, nw, w_in, wdt, conv_w, conv_b, d_inner, seqlen):
    t, d = x.shape
    conv_dim = conv_w.shape[1]
    gn = (conv_dim - d_inner) // 2
    assert d_inner % CONV_COL_BLOCK == 0 and gn % CONV_COL_BLOCK == 0
    tm = TOKEN_TILE
    row = lambda n: pl.BlockSpec((tm, n), lambda i: (i, 0))
    return pl.pallas_call(
        functools.partial(_ssm_in_body, tiles_per_seq=seqlen // tm, d_inner=d_inner, gn=gn),
        grid=(t // tm,),
        in_specs=[row(d), _resident((1, d)), _resident(w_in.shape),
                  _resident((d, LANES)), _resident((CONV_WIDTH, conv_dim)), _resident((1, conv_dim))],
        out_specs=[row(d_inner), row(d_inner), row(gn), row(gn), row(LANES)],
        out_shape=[jax.ShapeDtypeStruct((t, d_inner), F32), jax.ShapeDtypeStruct((t, d_inner), F32),
                   jax.ShapeDtypeStruct((t, gn), BF16), jax.ShapeDtypeStruct((t, gn), BF16),
                   jax.ShapeDtypeStruct((t, LANES), F32)],
        scratch_shapes=[pltpu.VMEM((conv_dim // LANES, SUBLANES + tm, LANES), F32)],
        compiler_params=_params(("arbitrary",)),
        name="ssm_in",
    )(x, nw.reshape(1, d), w_in, wdt, conv_w, conv_b.reshape(1, conv_dim))


def _cumsum_rows(x):
    n = x.shape[0]
    row = lax.broadcasted_iota(jnp.int32, x.shape, 0)
    step = 1
    while step < n:
        x = x + jnp.where(row >= step, pltpu.roll(x, step, axis=0), 0.0)
        step *= 2
    return x


def _expand_heads(v, e, heads):
    valid = lax.broadcasted_iota(jnp.int32, v.shape, 1) < heads
    hi = v.astype(BF16).astype(F32)
    r1 = v - hi
    mid = r1.astype(BF16).astype(F32)
    lo = (r1 - mid).astype(BF16).astype(F32)
    keep = lambda p: jnp.where(valid, p, 0.0)
    packed = keep(hi) + pltpu.roll(keep(mid), heads, axis=1) + pltpu.roll(keep(lo), 2 * heads, axis=1)
    return jnp.dot(packed.astype(BF16), e, preferred_element_type=F32)


def _cast_specs(arrays, n_steps, step_of):
    specs, shapes = [], []
    for arr in arrays:
        rows = arr.shape[0] // n_steps
        assert rows * n_steps == arr.shape[0] and rows % (2 * SUBLANES) == 0
        specs.append(pl.BlockSpec((rows, arr.shape[1]), lambda *idx: (step_of(*idx), 0)))
        shapes.append(jax.ShapeDtypeStruct(arr.shape, BF16))
    return specs, shapes


def _ssd_body(gate_ref, xs_ref, b_ref, c_ref, dt_ref, dtb_ref, alog_ref, dskip_ref, nw_ref, e_ref, *rest):
    n_cast = (len(rest) - 2) // 2
    cast_in, o_ref, cast_out, state_ref = rest[:n_cast], rest[n_cast], rest[n_cast + 1:-1], rest[-1]
    for src, dst in zip(cast_in, cast_out):
        dst[...] = src[...].astype(BF16)

    L = CHUNK
    n_state = SSM_STATE
    pair = 2 * SSM_HEAD_DIM
    d_inner = xs_ref.shape[-1]
    heads = d_inner // SSM_HEAD_DIM
    n_pairs = d_inner // pair
    pairs_per_group = n_pairs // SSM_GROUPS
    group_w = d_inner // SSM_GROUPS
    seqs = range(SEQS_PER_STEP)

    @pl.when(pl.program_id(1) == 0)
    def _():
        state_ref[...] = jnp.zeros_like(state_ref)

    row = lax.broadcasted_iota(jnp.int32, (L, L), 0)
    col = lax.broadcasted_iota(jnp.int32, (L, L), 1)
    causal = col <= row
    low_half = lax.broadcasted_iota(jnp.int32, (L, pair), 1) < SSM_HEAD_DIM
    group = lambda ref, seq, g: ref[seq, :, g * n_state:(g + 1) * n_state]

    def prepare(seq):
        dt = _softplus(dt_ref[seq] + dtb_ref[...])
        acs = _cumsum_rows(dt * (-jnp.exp(alog_ref[...]) * LOG2_E))
        acs_t = acs.T
        dt_t = dt.T
        cb, b_t = [], []
        for g in range(SSM_GROUPS):
            bg = group(b_ref, seq, g)
            cb.append(lax.dot_general(group(c_ref, seq, g), bg, (((1,), (1,)), ((), ())),
                                      preferred_element_type=F32))
            b_t.append(bg.astype(F32).T)
        return dict(
            acs=acs, cb=cb, b_t=b_t,
            e_out=_expand_heads(jnp.exp2(acs), e_ref[...], heads),
            w_state_t=jnp.exp2(acs_t[:, L - 1:L] - acs_t) * dt_t,
            src_t=acs_t - jnp.log2(dt_t))

    def head_pair(seq, p, j):
        g = j // pairs_per_group
        m_blocks, bt_blocks = [], []
        for h in (2 * j, 2 * j + 1):
            seg = p["acs"][:, h:h + 1] - p["src_t"][h:h + 1, :]
            m_blocks.append((p["cb"][g] * jnp.exp2(jnp.where(causal, seg, -jnp.inf))).astype(BF16))
            bt_blocks.append((p["b_t"][g] * p["w_state_t"][h:h + 1, :]).astype(BF16))
        lhs = jnp.concatenate([jnp.concatenate(m_blocks, axis=1), jnp.concatenate(bt_blocks, axis=1)], axis=0)
        xb = xs_ref[seq, :, j * pair:(j + 1) * pair]
        rhs = jnp.concatenate([jnp.where(low_half, xb, 0.0), jnp.where(low_half, 0.0, xb)], axis=0).astype(BF16)
        res = jnp.dot(lhs, rhs, preferred_element_type=F32)
        return res[:L], res[L:]

    def finish(seq, p, y_diag, d_state):
        state = state_ref[seq]
        y_off = [jnp.dot(group(c_ref, seq, g), state[:, g * group_w:(g + 1) * group_w].astype(BF16),
                         preferred_element_type=F32) for g in range(SSM_GROUPS)]
        state_ref[seq] = state * p["e_out"][L - 1:L, :] + jnp.concatenate(d_state, axis=1)
        y = (jnp.concatenate(y_diag, axis=1) + jnp.concatenate(y_off, axis=1) * p["e_out"]
             + xs_ref[seq] * dskip_ref[...])
        y = y * gate_ref[seq]
        normed = []
        for g in range(SSM_GROUPS):
            yg = y[:, g * group_w:(g + 1) * group_w]
            normed.append(yg * lax.rsqrt(jnp.mean(yg * yg, axis=-1, keepdims=True) + EPS))
        o_ref[seq] = (jnp.concatenate(normed, axis=1) * nw_ref[...]).astype(BF16)

    prepared = [prepare(seq) for seq in seqs]
    results = [([], []) for _ in seqs]
    for j in range(n_pairs):
        for seq in seqs:
            y_j, s_j = head_pair(seq, prepared[seq], j)
            results[seq][0].append(y_j)
            results[seq][1].append(s_j)
    for seq in seqs:
        finish(seq, prepared[seq], *results[seq])


def _ssd(gate, xs, b, c, dt_raw, dt_bias, a_log, d_skip, norm_w, batch, seqlen, cast=()):
    t, d_inner = xs.shape
    gn = b.shape[1]
    heads = d_inner // SSM_HEAD_DIM
    assert 3 * heads <= LANES and batch % SEQS_PER_STEP == 0
    nc = seqlen // CHUNK
    pad = lambda v: jnp.pad(v.reshape(1, heads), ((0, 0), (0, LANES - heads)))
    piece_head = jnp.where(jnp.arange(LANES) < 3 * heads, jnp.arange(LANES) % heads, -1)
    expand = (piece_head[:, None] == (jnp.arange(d_inner) // SSM_HEAD_DIM)[None, :]).astype(BF16)
    seq3 = lambda v: v.reshape(batch, seqlen, v.shape[1])
    blk = lambda n: pl.BlockSpec((SEQS_PER_STEP, CHUNK, n), lambda bi, ci: (bi, ci, 0))
    groups = batch // SEQS_PER_STEP
    cast_specs, cast_shapes = _cast_specs(cast, groups * nc, lambda bi, ci: bi * nc + ci)
    out, *cast_out = pl.pallas_call(
        _ssd_body,
        grid=(groups, nc),
        in_specs=[blk(d_inner), blk(d_inner), blk(gn), blk(gn), blk(LANES),
                  _resident((1, LANES)), _resident((1, LANES)), _resident((1, d_inner)),
                  _resident((1, d_inner)), _resident((LANES, d_inner))] + cast_specs,
        out_specs=[blk(d_inner)] + cast_specs,
        out_shape=[jax.ShapeDtypeStruct((batch, seqlen, d_inner), BF16)] + cast_shapes,
        scratch_shapes=[pltpu.VMEM((SEQS_PER_STEP, SSM_STATE, d_inner), F32)],
        compiler_params=_params(("arbitrary", "arbitrary")),
        name="ssd",
    )(seq3(gate), seq3(xs), seq3(b), seq3(c), seq3(dt_raw), pad(dt_bias), pad(a_log),
      jnp.repeat(d_skip, SSM_HEAD_DIM).reshape(1, d_inner), norm_w.reshape(1, d_inner), expand, *cast)
    return out.reshape(t, d_inner), cast_out


def _attn_body(sink_ref, q_ref, kc_ref, kp_ref, vc_ref, vp_ref, o_ref):
    W = WINDOW
    n_blocks = q_ref.shape[1] // LANES
    blocks_per_kv = n_blocks // N_KV_HEADS
    row = lax.broadcasted_iota(jnp.int32, (W, 2 * W), 0)
    col = lax.broadcasted_iota(jnp.int32, (W, 2 * W), 1)
    in_window = (col <= row + W) & (col > row)

    def band(cur_ref, prev_ref, qi, kh):
        rows = slice(qi * W, (qi + 1) * W)
        parts = []
        for half in range(2):
            cs = slice((2 * kh + half) * LANES, (2 * kh + half + 1) * LANES)
            parts.append(prev_ref[:, cs] if qi == 0 else cur_ref[(qi - 1) * W:qi * W, cs])
            parts.append(cur_ref[rows, cs])
        return jnp.concatenate(parts, axis=0)

    scores = {}
    for qi in range(Q_BLOCKS_PER_STEP):
        rows = slice(qi * W, (qi + 1) * W)
        for kh in range(N_KV_HEADS):
            q_rows = jnp.concatenate([q_ref[rows, (kh * blocks_per_kv + jj) * LANES:(kh * blocks_per_kv + jj + 1) * LANES]
                                      for jj in range(blocks_per_kv)], axis=0)
            scores[qi, kh] = lax.dot_general(q_rows, band(kc_ref, kp_ref, qi, kh), (((1,), (1,)), ((), ())),
                                             preferred_element_type=F32)

    for qi in range(Q_BLOCKS_PER_STEP):
        rows = slice(qi * W, (qi + 1) * W)
        mask = in_window
        if qi == 0:
            mask = mask & (col >= jnp.where(pl.program_id(1) > 0, 0, W))
        for kh in range(N_KV_HEADS):
            s = scores[qi, kh]
            probs = []
            for jj in range(blocks_per_kv):
                halves = []
                for t in range(2):
                    sink = sink_ref[2 * (kh * blocks_per_kv + jj) + t]
                    sh = jnp.where(mask, s[jj * W:(jj + 1) * W, t * 2 * W:(t + 1) * 2 * W], -jnp.inf)
                    m = jnp.maximum(jnp.max(sh, axis=-1, keepdims=True), sink)
                    p = jnp.exp(sh - m)
                    denom = jnp.sum(p, axis=-1, keepdims=True) + jnp.exp(sink - m)
                    halves.append((p / denom).astype(BF16))
                probs.append(jnp.concatenate(halves, axis=1))
            o = jnp.dot(jnp.concatenate(probs, axis=0), band(vc_ref, vp_ref, qi, kh), preferred_element_type=F32)
            for jj in range(blocks_per_kv):
                j = kh * blocks_per_kv + jj
                o_ref[rows, j * LANES:(j + 1) * LANES] = o[jj * W:(jj + 1) * W].astype(BF16)


def _attn(sinks, q, k, v, batch, seqlen):
    t, nq = q.shape
    nkv = k.shape[1]
    rows = Q_BLOCKS_PER_STEP * WINDOW
    steps = seqlen // rows
    cur = lambda w: pl.BlockSpec((rows, w), lambda bi, ni: (bi * steps + ni, 0))
    prev = lambda w: pl.BlockSpec(
        (WINDOW, w), lambda bi, ni: ((bi * steps + ni) * Q_BLOCKS_PER_STEP - jnp.where(ni > 0, 1, 0), 0))
    return pl.pallas_call(
        _attn_body,
        grid=(batch, steps),
        in_specs=[pl.BlockSpec(memory_space=pltpu.SMEM), cur(nq), cur(nkv), prev(nkv), cur(nkv), prev(nkv)],
        out_specs=cur(nq),
        out_shape=jax.ShapeDtypeStruct((t, nq), BF16),
        compiler_params=_params(("parallel", "parallel")),
        name="attn",
    )(sinks, q, k, k, v, v)


def kernel(x, norm_w, ffn_w_gate, ffn_w_up, ffn_w_down, ssm_w_in, ssm_conv_w, ssm_conv_b, ssm_dt_bias, ssm_a_log, ssm_d, ssm_norm_w, ssm_w_out, kv_norm_w, w_k, b_k, w_v, b_v, attn_w_q, attn_b_q, attn_sinks, attn_w_o, attn_b_o, final_norm_w):
    batch, seqlen, d = x.shape
    assert seqlen % TOKEN_TILE == 0 and seqlen % CHUNK == 0 and seqlen % (Q_BLOCKS_PER_STEP * WINDOW) == 0
    assert ssm_w_in.shape[0] == 1 and attn_w_q.shape[0] == 1 and norm_w.shape[0] == 2
    d_inner = ssm_w_out.shape[1]
    heads = d_inner // SSM_HEAD_DIM
    conv_dim = ssm_conv_w.shape[-1]
    bf = lambda w: w.astype(BF16)
    xt = x.reshape(batch * seqlen, d)
    cos, sin = _rope_tables(seqlen)

    ffn_stacks = (ffn_w_gate, ffn_w_up, ffn_w_down)

    (xt,) = _block(xt, norm_w[0, 0], tuple(bf(w[0, 0]) for w in ffn_stacks), ())
    w_in = bf(ssm_w_in[0])
    w_dt = jnp.pad(w_in[:, d_inner + conv_dim:], ((0, 0), (0, LANES - heads)))
    gate, xs, b, c, dt_raw = _ssm_in(xt, norm_w[0, 1], w_in, w_dt, ssm_conv_w[0], ssm_conv_b[0], d_inner, seqlen)
    yn, ffn_bf = _ssd(gate, xs, b, c, dt_raw, ssm_dt_bias[0], ssm_a_log[0], ssm_d[0], ssm_norm_w[0], batch, seqlen,
                      cast=tuple(w.reshape(-1, w.shape[-1]) for w in ffn_stacks))
    ffn_weights = tuple(w_bf.reshape(w.shape) for w_bf, w in zip(ffn_bf, ffn_stacks))

    def block(v, layer, idx, **kw):
        return _block(v, norm_w[layer, 2 * idx], ffn_weights, (layer, idx), seqlen=seqlen, **kw)

    xt, k, v = block(xt, 0, 1, pre=(yn, bf(ssm_w_out[0]), None),
                     post=("kv", kv_norm_w, bf(w_k), b_k, bf(w_v), b_v, cos, sin))

    xt, q = block(xt, 1, 0, post=("q", norm_w[1, 1], bf(attn_w_q[0]), attn_b_q[0], cos, sin))
    att = _attn(attn_sinks[0], q, k, v, batch, seqlen)
    (xt,) = block(xt, 1, 1, pre=(att, bf(attn_w_o[0]), attn_b_o[0]), final_nw=final_norm_w)
    return xt.reshape(batch, seqlen, d)
```

```python
import functools
import math

import jax
import jax.numpy as jnp
from jax import lax
from jax.experimental import pallas as pl
from jax.experimental.pallas import tpu as pltpu

F32 = jnp.float32
BF16 = jnp.bfloat16

SSM_HEAD_DIM = 64
SSM_GROUPS = 4
SSM_STATE = 128
CONV_WIDTH = 4
CHUNK = 128
ATT_HEAD_DIM = 64
N_KV_HEADS = 4
WINDOW = 128
ROPE_THETA = 10000.0
FFN_RES_WEIGHT = 0.5
EPS = 1e-5
LOG2_E = 1.0 / math.log(2.0)

LANES = 128
SUBLANES = 8
VMEM_LIMIT_BYTES = 56 * 1024 * 1024

TOKEN_TILE = 512
BLOCK_SUBTILES = 2
CONV_COL_BLOCK = 512
CONV_ROW_STRIP = 32
SEQS_PER_STEP = 2


def _rmsnorm(x, w):
    return x * lax.rsqrt(jnp.mean(x * x, axis=-1, keepdims=True) + EPS) * w


def _silu(x):
    return x / (1.0 + jnp.exp2(x * -LOG2_E))


def _softplus(x):
    return jnp.maximum(x, 0.0) + jnp.log(1.0 + jnp.exp(-jnp.abs(x)))


def _resident(shape):
    zeros = (0,) * len(shape)
    return pl.BlockSpec(shape, lambda *_: zeros, pipeline_mode=pl.Buffered(1))


def _params(semantics):
    return pltpu.CompilerParams(dimension_semantics=semantics, vmem_limit_bytes=VMEM_LIMIT_BYTES)


def _rope_tables(seqlen):
    pos = jnp.arange(seqlen, dtype=F32)
    inv = 1.0 / (ROPE_THETA ** (jnp.arange(0, ATT_HEAD_DIM, 2, dtype=F32) / ATT_HEAD_DIM))
    ang = pos[:, None] * inv[None, :]
    cos, sin = jnp.cos(ang), jnp.sin(ang)
    reps = LANES // ATT_HEAD_DIM
    return (jnp.tile(jnp.concatenate([cos, cos], axis=-1), (1, reps)),
            jnp.tile(jnp.concatenate([-sin, sin], axis=-1), (1, reps)))


def _lane_blocks(t):
    return [t[:, j * LANES:(j + 1) * LANES] for j in range(t.shape[1] // LANES)]


def _rope(t, cos, sin_signed):
    half = ATT_HEAD_DIM // 2
    lane = lax.broadcasted_iota(jnp.int32, (t.shape[0], LANES), 1)
    first_half = (lane % ATT_HEAD_DIM) < half
    out = []
    for blk in _lane_blocks(t):
        partner = jnp.where(first_half, pltpu.roll(blk, LANES - half, axis=1), pltpu.roll(blk, half, axis=1))
        out.append(blk * cos + partner * sin_signed)
    return jnp.concatenate(out, axis=1)


def _separate_heads(t):
    low_half = lax.broadcasted_iota(jnp.int32, (t.shape[0], LANES), 1) < ATT_HEAD_DIM
    out = []
    for blk in _lane_blocks(t):
        swapped = pltpu.roll(blk, ATT_HEAD_DIM, axis=1)
        out += [jnp.where(low_half, blk, 0.0), jnp.where(low_half, 0.0, swapped),
                jnp.where(low_half, swapped, 0.0), jnp.where(low_half, 0.0, blk)]
    return jnp.concatenate(out, axis=1)


def _block_body(*refs, has_pre, has_pre_bias, post, final_norm, scale, n_cast, attn):
    if attn is not None:
        *refs, att_ref = refs
    it = iter(refs)
    x_ref = next(it)
    if attn is not None:
        sink_ref, q_ref, kc_ref, kp_ref, vc_ref, vp_ref = (next(it) for _ in range(6))
        a_ref, wp_ref, bp_ref = att_ref, next(it), next(it) if has_pre_bias else None
    else:
        a_ref, wp_ref, bp_ref = (next(it), next(it), next(it) if has_pre_bias else None) if has_pre else (None,) * 3
    nw_ref, wg_ref, wu_ref, wd_ref = next(it), next(it), next(it), next(it)
    if post == "kv":
        pnw_ref, wk_ref, bk_ref, wv_ref, bv_ref, cos_ref, sin_ref = (next(it) for _ in range(7))
    elif post == "q":
        pnw_ref, wq_ref, bq_ref, cos_ref, sin_ref = (next(it) for _ in range(5))
    fnw_ref = next(it) if final_norm else None
    cast_in = [next(it) for _ in range(n_cast)]
    o_ref = next(it)
    outs = list(it)
    post_out, cast_out = outs[:len(outs) - n_cast], outs[len(outs) - n_cast:]
    for src, dst in zip(cast_in, cast_out):
        dst[...] = src[...].astype(BF16)

    sub = x_ref.shape[0] // BLOCK_SUBTILES
    dot = lambda a, w_ref: jnp.dot(a, w_ref[...], preferred_element_type=F32)

    def load(s, rows):
        s["x"] = x_ref[rows]
        if has_pre:
            s["x"] = s["x"] + dot(a_ref[rows], wp_ref)
            if has_pre_bias:
                s["x"] = s["x"] + bp_ref[...]

    def norm(s, rows):
        s["h"] = _rmsnorm(s["x"], nw_ref[...]).astype(BF16)

    def gate_up(s, rows):
        s["g"] = dot(s["h"], wg_ref)
        s["u"] = dot(s["h"], wu_ref)

    def activate(s, rows):
        s["a"] = (_silu(s["g"]) * s["u"]).astype(BF16)

    def down(s, rows):
        s["y"] = s["x"] + FFN_RES_WEIGHT * dot(s["a"], wd_ref)

    def post_norm(s, rows):
        s["hp"] = _rmsnorm(s["y"], pnw_ref[...]).astype(BF16)

    def post_project(s, rows):
        if post == "kv":
            s["k"] = dot(s["hp"], wk_ref) + bk_ref[...]
            s["v"] = dot(s["hp"], wv_ref) + bv_ref[...]
        else:
            s["q"] = dot(s["hp"], wq_ref) + bq_ref[...]

    def store(s, rows):
        o_ref[rows] = _rmsnorm(s["y"], fnw_ref[...]) if final_norm else s["y"]
        if post == "kv":
            post_out[0][rows] = _separate_heads(_rope(s["k"], cos_ref[rows], sin_ref[rows])).astype(BF16)
            post_out[1][rows] = _separate_heads(s["v"]).astype(BF16)
        elif post == "q":
            post_out[0][rows] = (_rope(s["q"], cos_ref[rows], sin_ref[rows]) * scale).astype(BF16)

    states = [{} for _ in range(BLOCK_SUBTILES)]
    run = lambda stage, i: stage(states[i], slice(i * sub, (i + 1) * sub))
    every = lambda stage: [run(stage, i) for i in range(BLOCK_SUBTILES)]

    if attn is None:
        for stage in [load, norm, gate_up, activate, down] + ([post_norm, post_project] if post else []) + [store]:
            every(stage)
        return

    n_tiles, tiles_per_seq = attn
    step = pl.program_id(0)

    @pl.when(step == 0)
    def _():
        att_ref[...] = jnp.zeros_like(att_ref)

    has_prev = jnp.minimum(step, n_tiles - 1) % tiles_per_seq > 0
    q_blocks = list(range(q_ref.shape[0] // WINDOW))
    halves = [q_blocks[:len(q_blocks) // 2], q_blocks[len(q_blocks) // 2:]]
    every(load)
    scores = _attn_scores(q_ref, kc_ref, kp_ref, halves[0])
    every(norm)
    run(gate_up, 0)
    _attn_outputs(scores, sink_ref, vc_ref, vp_ref, has_prev, att_ref)
    scores = _attn_scores(q_ref, kc_ref, kp_ref, halves[1])
    for i in range(1, BLOCK_SUBTILES):
        run(gate_up, i)
    _attn_outputs(scores, sink_ref, vc_ref, vp_ref, has_prev, att_ref)
    for stage in [activate, down, store]:
        every(stage)


def _block(x, nw, ffn_weights, ffn_index, *, pre=None, post=None, final_nw=None, seqlen=None, cast=()):
    t, d = x.shape
    tm = TOKEN_TILE
    n_tiles = t // tm
    attn = pre is not None and pre[0] == "attn"
    row = lambda w: pl.BlockSpec((tm, w), (lambda i: (jnp.maximum(i - 1, 0), 0)) if attn else (lambda i: (i, 0)))
    vec = lambda v: v.reshape(1, -1)
    args, in_specs = [x], [row(d)]

    def add_resident(*arrays):
        for arr in arrays:
            args.append(arr)
            in_specs.append(_resident(arr.shape))

    if attn:
        _, sinks, q, k, v, wp, bp = pre
        tiles_per_seq = seqlen // tm
        tile = lambda i: jnp.minimum(i, n_tiles - 1)
        cur = lambda w: pl.BlockSpec((tm, w), lambda i: (tile(i), 0))
        prev = lambda w: pl.BlockSpec(
            (WINDOW, w), lambda i: (tile(i) * (tm // WINDOW) - jnp.where(tile(i) % tiles_per_seq > 0, 1, 0), 0))
        args += [sinks, q, k, k, v, v]
        in_specs += [pl.BlockSpec(memory_space=pltpu.SMEM), cur(q.shape[1]), cur(k.shape[1]), prev(k.shape[1]),
                     cur(v.shape[1]), prev(v.shape[1])]
        add_resident(wp, vec(bp))
    elif pre is not None:
        a, wp, bp = pre
        args.append(a)
        in_specs.append(row(a.shape[1]))
        add_resident(wp)
        if bp is not None:
            add_resident(vec(bp))
    add_resident(vec(nw))
    n_lead = len(ffn_index)
    for w in ffn_weights:
        args.append(w)
        in_specs.append(pl.BlockSpec((None,) * n_lead + w.shape[n_lead:],
                                     lambda *_: tuple(ffn_index) + (0,) * (w.ndim - n_lead),
                                     pipeline_mode=pl.Buffered(1)))
    out_specs = [row(d)]
    out_shape = [jax.ShapeDtypeStruct((t, d), F32)]
    kind = None
    if post is not None:
        kind = post[0]
        table = pl.BlockSpec((tm, LANES), lambda i: (i % (seqlen // tm), 0))
        if kind == "kv":
            _, pnw, wk, bk, wv, bv, cos, sin = post
            add_resident(vec(pnw), wk, vec(bk), wv, vec(bv))
            sep = 2 * (LANES // ATT_HEAD_DIM) * wk.shape[1]
            out_specs += [row(sep), row(sep)]
            out_shape += [jax.ShapeDtypeStruct((t, sep), BF16)] * 2
        else:
            _, pnw, wq, bq, cos, sin = post
            add_resident(vec(pnw), wq, vec(bq))
            out_specs.append(row(wq.shape[1]))
            out_shape.append(jax.ShapeDtypeStruct((t, wq.shape[1]), BF16))
        args += [cos, sin]
        in_specs += [table, table]
    if final_nw is not None:
        add_resident(vec(final_nw))
    cast_specs, cast_shapes = _cast_specs(cast, t // tm, lambda i: i)
    args += list(cast)
    in_specs += cast_specs
    out_specs += cast_specs
    out_shape += cast_shapes
    body = functools.partial(_block_body, has_pre=pre is not None, has_pre_bias=pre is not None and pre[-1] is not None,
                             post=kind, final_norm=final_nw is not None, scale=LOG2_E / math.sqrt(ATT_HEAD_DIM),
                             n_cast=len(cast), attn=(n_tiles, tiles_per_seq) if attn else None)
    return pl.pallas_call(
        body,
        grid=(n_tiles + 1 if attn else n_tiles,),
        in_specs=in_specs,
        out_specs=out_specs,
        out_shape=out_shape,
        scratch_shapes=[pltpu.VMEM((tm, q.shape[1]), BF16)] if attn else [],
        compiler_params=_params(("arbitrary",) if attn else ("parallel",)),
        name="block",
    )(*args)


def _ssm_in_body(x_ref, nw_ref, w_ref, wdt_ref, cw_ref, cb_ref,
                 gate_ref, xs_ref, b_ref, c_ref, dt_ref, u_ref, *, tiles_per_seq, d_inner, gn):
    tm = x_ref.shape[0]
    conv_dim = d_inner + 2 * gn
    cblk = CONV_COL_BLOCK
    n_blk = conv_dim // cblk

    @pl.when(pl.program_id(0) % tiles_per_seq == 0)
    def _():
        u_ref[:, :SUBLANES, :] = jnp.zeros((u_ref.shape[0], SUBLANES, LANES), F32)

    h = _rmsnorm(x_ref[...], nw_ref[...]).astype(BF16)

    def project(blk):
        return jnp.dot(h, w_ref[:, d_inner + blk * cblk:d_inner + (blk + 1) * cblk], preferred_element_type=F32)

    def conv(u, blk):
        lo = blk * cblk
        for s in range(cblk // LANES):
            slab = blk * (cblk // LANES) + s
            cs = slice(lo + s * LANES, lo + (s + 1) * LANES)
            u_ref[slab, SUBLANES:, :] = u[:, s * LANES:(s + 1) * LANES]
            for r in range(0, tm, CONV_ROW_STRIP):
                rows = slice(r, r + CONV_ROW_STRIP)
                acc = cb_ref[:, cs]
                for k in range(CONV_WIDTH):
                    shift = CONV_WIDTH - 1 - k
                    acc = acc + cw_ref[k:k + 1, cs] * u_ref[slab, pl.ds(SUBLANES - shift + r, CONV_ROW_STRIP), :]
                act = _silu(acc)
                if lo < d_inner:
                    xs_ref[rows, cs] = act
                elif lo < d_inner + gn:
                    b_ref[rows, lo - d_inner + s * LANES:lo - d_inner + (s + 1) * LANES] = act.astype(BF16)
                else:
                    off = lo - d_inner - gn
                    c_ref[rows, off + s * LANES:off + (s + 1) * LANES] = act.astype(BF16)
            u_ref[slab, :SUBLANES, :] = u_ref[slab, tm:, :]

    u_next = project(0)
    for blk in range(n_blk):
        u = u_next
        if blk + 1 < n_blk:
            u_next = project(blk + 1)
        else:
            gate_ref[...] = _silu(jnp.dot(h, w_ref[:, :d_inner], preferred_element_type=F32))
            dt_ref[...] = jnp.dot(h, wdt_ref[...], preferred_element_type=F32)
        conv(u, blk)


def _ssm_in(x, nw, w_in, wdt, conv_w, conv_b, d_inner, seqlen):
    t, d = x.shape
    conv_dim = conv_w.shape[1]
    gn = (conv_dim - d_inner) // 2
    assert d_inner % CONV_COL_BLOCK == 0 and gn % CONV_COL_BLOCK == 0
    tm = TOKEN_TILE
    row = lambda n: pl.BlockSpec((tm, n), lambda i: (i, 0))
    return pl.pallas_call(
        functools.partial(_ssm_in_body, tiles_per_seq=seqlen // tm, d_inner=d_inner, gn=gn),
        grid=(t // tm,),
        in_specs=[row(d), _resident((1, d)), _resident(w_in.shape),
                  _resident((d, LANES)), _resident((CONV_WIDTH, conv_dim)), _resident((1, conv_dim))],
        out_specs=[row(d_inner), row(d_inner), row(gn), row(gn), row(LANES)],
        out_shape=[jax.ShapeDtypeStruct((t, d_inner), F32), jax.ShapeDtypeStruct((t, d_inner), F32),
                   jax.ShapeDtypeStruct((t, gn), BF16), jax.ShapeDtypeStruct((t, gn), BF16),
                   jax.ShapeDtypeStruct((t, LANES), F32)],
        scratch_shapes=[pltpu.VMEM((conv_dim // LANES, SUBLANES + tm, LANES), F32)],
        compiler_params=_params(("arbitrary",)),
        name="ssm_in",
    )(x, nw.reshape(1, d), w_in, wdt, conv_w, conv_b.reshape(1, conv_dim))


def _cumsum_rows(x):
    n = x.shape[0]
    row = lax.broadcasted_iota(jnp.int32, x.shape, 0)
    step = 1
    while step < n:
        x = x + jnp.where(row >= step, pltpu.roll(x, step, axis=0), 0.0)
        step *= 2
    return x


def _expand_heads(v, e, heads):
    valid = lax.broadcasted_iota(jnp.int32, v.shape, 1) < heads
    hi = v.astype(BF16).astype(F32)
    r1 = v - hi
    mid = r1.astype(BF16).astype(F32)
    lo = (r1 - mid).astype(BF16).astype(F32)
    keep = lambda p: jnp.where(valid, p, 0.0)
    packed = keep(hi) + pltpu.roll(keep(mid), heads, axis=1) + pltpu.roll(keep(lo), 2 * heads, axis=1)
    return jnp.dot(packed.astype(BF16), e, preferred_element_type=F32)


def _cast_specs(arrays, n_steps, step_of):
    specs, shapes = [], []
    for arr in arrays:
        rows = arr.shape[0] // n_steps
        assert rows * n_steps == arr.shape[0] and rows % (2 * SUBLANES) == 0
        specs.append(pl.BlockSpec((rows, arr.shape[1]), lambda *idx: (step_of(*idx), 0)))
        shapes.append(jax.ShapeDtypeStruct(arr.shape, BF16))
    return specs, shapes


def _ssd_body(gate_ref, xs_ref, b_ref, c_ref, dt_ref, dtb_ref, alog_ref, dskip_ref, nw_ref, e_ref, *rest):
    n_cast = (len(rest) - 2) // 2
    cast_in, o_ref, cast_out, state_ref = rest[:n_cast], rest[n_cast], rest[n_cast + 1:-1], rest[-1]
    for src, dst in zip(cast_in, cast_out):
        dst[...] = src[...].astype(BF16)

    L = CHUNK
    n_state = SSM_STATE
    pair = 2 * SSM_HEAD_DIM
    d_inner = xs_ref.shape[-1]
    heads = d_inner // SSM_HEAD_DIM
    n_pairs = d_inner // pair
    pairs_per_group = n_pairs // SSM_GROUPS
    group_w = d_inner // SSM_GROUPS
    seqs = range(SEQS_PER_STEP)

    @pl.when(pl.program_id(1) == 0)
    def _():
        state_ref[...] = jnp.zeros_like(state_ref)

    row = lax.broadcasted_iota(jnp.int32, (L, L), 0)
    col = lax.broadcasted_iota(jnp.int32, (L, L), 1)
    causal = col <= row
    low_half = lax.broadcasted_iota(jnp.int32, (L, pair), 1) < SSM_HEAD_DIM
    group = lambda ref, seq, g: ref[seq, :, g * n_state:(g + 1) * n_state]

    def prepare(seq):
        dt = _softplus(dt_ref[seq] + dtb_ref[...])
        acs = _cumsum_rows(dt * (-jnp.exp(alog_ref[...]) * LOG2_E))
        acs_t = acs.T
        dt_t = dt.T
        cb, b_t = [], []
        for g in range(SSM_GROUPS):
            bg = group(b_ref, seq, g)
            cb.append(lax.dot_general(group(c_ref, seq, g), bg, (((1,), (1,)), ((), ())),
                                      preferred_element_type=F32))
            b_t.append(bg.astype(F32).T)
        return dict(
            acs=acs, cb=cb, b_t=b_t,
            e_out=_expand_heads(jnp.exp2(acs), e_ref[...], heads),
            w_state_t=jnp.exp2(acs_t[:, L - 1:L] - acs_t) * dt_t,
            src_t=acs_t - jnp.log2(dt_t))

    def head_pair(seq, p, j):
        g = j // pairs_per_group
        m_blocks, bt_blocks = [], []
        for h in (2 * j, 2 * j + 1):
            seg = p["acs"][:, h:h + 1] - p["src_t"][h:h + 1, :]
            m_blocks.append((p["cb"][g] * jnp.exp2(jnp.where(causal, seg, -jnp.inf))).astype(BF16))
            bt_blocks.append((p["b_t"][g] * p["w_state_t"][h:h + 1, :]).astype(BF16))
        lhs = jnp.concatenate([jnp.concatenate(m_blocks, axis=1), jnp.concatenate(bt_blocks, axis=1)], axis=0)
        xb = xs_ref[seq, :, j * pair:(j + 1) * pair]
        rhs = jnp.concatenate([jnp.where(low_half, xb, 0.0), jnp.where(low_half, 0.0, xb)], axis=0).astype(BF16)
        res = jnp.dot(lhs, rhs, preferred_element_type=F32)
        return res[:L], res[L:]

    def finish(seq, p, y_diag, d_state):
        state = state_ref[seq]
        y_off = [jnp.dot(group(c_ref, seq, g), state[:, g * group_w:(g + 1) * group_w].astype(BF16),
                         preferred_element_type=F32) for g in range(SSM_GROUPS)]
        state_ref[seq] = state * p["e_out"][L - 1:L, :] + jnp.concatenate(d_state, axis=1)
        y = (jnp.concatenate(y_diag, axis=1) + jnp.concatenate(y_off, axis=1) * p["e_out"]
             + xs_ref[seq] * dskip_ref[...])
        y = y * gate_ref[seq]
        normed = []
        for g in range(SSM_GROUPS):
            yg = y[:, g * group_w:(g + 1) * group_w]
            normed.append(yg * lax.rsqrt(jnp.mean(yg * yg, axis=-1, keepdims=True) + EPS))
        o_ref[seq] = (jnp.concatenate(normed, axis=1) * nw_ref[...]).astype(BF16)

    prepared = [prepare(seq) for seq in seqs]
    results = [([], []) for _ in seqs]
    for j in range(n_pairs):
        for seq in seqs:
            y_j, s_j = head_pair(seq, prepared[seq], j)
            results[seq][0].append(y_j)
            results[seq][1].append(s_j)
    for seq in seqs:
        finish(seq, prepared[seq], *results[seq])


def _ssd(gate, xs, b, c, dt_raw, dt_bias, a_log, d_skip, norm_w, batch, seqlen, cast=()):
    t, d_inner = xs.shape
    gn = b.shape[1]
    heads = d_inner // SSM_HEAD_DIM
    assert 3 * heads <= LANES and batch % SEQS_PER_STEP == 0
    nc = seqlen // CHUNK
    pad = lambda v: jnp.pad(v.reshape(1, heads), ((0, 0), (0, LANES - heads)))
    piece_head = jnp.where(jnp.arange(LANES) < 3 * heads, jnp.arange(LANES) % heads, -1)
    expand = (piece_head[:, None] == (jnp.arange(d_inner) // SSM_HEAD_DIM)[None, :]).astype(BF16)
    seq3 = lambda v: v.reshape(batch, seqlen, v.shape[1])
    blk = lambda n: pl.BlockSpec((SEQS_PER_STEP, CHUNK, n), lambda bi, ci: (bi, ci, 0))
    groups = batch // SEQS_PER_STEP
    cast_specs, cast_shapes = _cast_specs(cast, groups * nc, lambda bi, ci: bi * nc + ci)
    out, *cast_out = pl.pallas_call(
        _ssd_body,
        grid=(groups, nc),
        in_specs=[blk(d_inner), blk(d_inner), blk(gn), blk(gn), blk(LANES),
                  _resident((1, LANES)), _resident((1, LANES)), _resident((1, d_inner)),
                  _resident((1, d_inner)), _resident((LANES, d_inner))] + cast_specs,
        out_specs=[blk(d_inner)] + cast_specs,
        out_shape=[jax.ShapeDtypeStruct((batch, seqlen, d_inner), BF16)] + cast_shapes,
        scratch_shapes=[pltpu.VMEM((SEQS_PER_STEP, SSM_STATE, d_inner), F32)],
        compiler_params=_params(("arbitrary", "arbitrary")),
        name="ssd",
    )(seq3(gate), seq3(xs), seq3(b), seq3(c), seq3(dt_raw), pad(dt_bias), pad(a_log),
      jnp.repeat(d_skip, SSM_HEAD_DIM).reshape(1, d_inner), norm_w.reshape(1, d_inner), expand, *cast)
    return out.reshape(t, d_inner), cast_out


def _key_band(cur_ref, prev_ref, qi, kh):
    W = WINDOW
    parts = []
    for half in range(2):
        cs = slice((2 * kh + half) * LANES, (2 * kh + half + 1) * LANES)
        parts.append(prev_ref[:, cs] if qi == 0 else cur_ref[(qi - 1) * W:qi * W, cs])
        parts.append(cur_ref[qi * W:(qi + 1) * W, cs])
    return jnp.concatenate(parts, axis=0)


def _attn_scores(q_ref, kc_ref, kp_ref, q_blocks):
    W = WINDOW
    blocks_per_kv = q_ref.shape[1] // LANES // N_KV_HEADS
    scores = {}
    for qi in q_blocks:
        rows = slice(qi * W, (qi + 1) * W)
        for kh in range(N_KV_HEADS):
            q_rows = jnp.concatenate([q_ref[rows, (kh * blocks_per_kv + jj) * LANES:(kh * blocks_per_kv + jj + 1) * LANES]
                                      for jj in range(blocks_per_kv)], axis=0)
            scores[qi, kh] = lax.dot_general(q_rows, _key_band(kc_ref, kp_ref, qi, kh), (((1,), (1,)), ((), ())),
                                             preferred_element_type=F32)
    return scores


def _attn_outputs(scores, sink_ref, vc_ref, vp_ref, has_prev, o_ref):
    W = WINDOW
    blocks_per_kv = o_ref.shape[1] // LANES // N_KV_HEADS
    row = lax.broadcasted_iota(jnp.int32, (W, 2 * W), 0)
    col = lax.broadcasted_iota(jnp.int32, (W, 2 * W), 1)
    in_window = (col <= row + W) & (col > row)
    low_half = lax.broadcasted_iota(jnp.int32, (W, LANES), 1) < ATT_HEAD_DIM
    for qi in sorted({qi for qi, _ in scores}):
        rows = slice(qi * W, (qi + 1) * W)
        mask = in_window
        if qi == 0:
            mask = mask & (col >= jnp.where(has_prev, 0, W))
        for kh in range(N_KV_HEADS):
            s = scores[qi, kh]
            weights, inv_sum = [], []
            for jj in range(blocks_per_kv):
                halves, inv = [], []
                for t in range(2):
                    sink = sink_ref[2 * (kh * blocks_per_kv + jj) + t] * LOG2_E
                    sh = jnp.where(mask, s[jj * W:(jj + 1) * W, t * 2 * W:(t + 1) * 2 * W], -jnp.inf)
                    m = jnp.maximum(jnp.max(sh, axis=-1, keepdims=True), sink)
                    p = jnp.exp2(sh - m)
                    inv.append(1.0 / (jnp.sum(p, axis=-1, keepdims=True) + jnp.exp2(sink - m)))
                    halves.append(p.astype(BF16))
                weights.append(jnp.concatenate(halves, axis=1))
                inv_sum.append(jnp.where(low_half, inv[0], inv[1]))
            o = jnp.dot(jnp.concatenate(weights, axis=0), _key_band(vc_ref, vp_ref, qi, kh),
                        preferred_element_type=F32)
            for jj in range(blocks_per_kv):
                j = kh * blocks_per_kv + jj
                o_ref[rows, j * LANES:(j + 1) * LANES] = (o[jj * W:(jj + 1) * W] * inv_sum[jj]).astype(BF16)


def kernel(x, norm_w, ffn_w_gate, ffn_w_up, ffn_w_down, ssm_w_in, ssm_conv_w, ssm_conv_b, ssm_dt_bias, ssm_a_log, ssm_d, ssm_norm_w, ssm_w_out, kv_norm_w, w_k, b_k, w_v, b_v, attn_w_q, attn_b_q, attn_sinks, attn_w_o, attn_b_o, final_norm_w):
    batch, seqlen, d = x.shape
    assert seqlen % TOKEN_TILE == 0 and TOKEN_TILE % (2 * WINDOW) == 0 and seqlen % CHUNK == 0
    assert ssm_w_in.shape[0] == 1 and attn_w_q.shape[0] == 1 and norm_w.shape[0] == 2
    d_inner = ssm_w_out.shape[1]
    heads = d_inner // SSM_HEAD_DIM
    conv_dim = ssm_conv_w.shape[-1]
    bf = lambda w: w.astype(BF16)
    xt = x.reshape(batch * seqlen, d)
    cos, sin = _rope_tables(seqlen)

    ffn_stacks = (ffn_w_gate, ffn_w_up, ffn_w_down)

    (xt,) = _block(xt, norm_w[0, 0], tuple(bf(w[0, 0]) for w in ffn_stacks), ())
    w_in = bf(ssm_w_in[0])
    w_dt = jnp.pad(w_in[:, d_inner + conv_dim:], ((0, 0), (0, LANES - heads)))
    gate, xs, b, c, dt_raw = _ssm_in(xt, norm_w[0, 1], w_in, w_dt, ssm_conv_w[0], ssm_conv_b[0], d_inner, seqlen)
    yn, ffn_bf = _ssd(gate, xs, b, c, dt_raw, ssm_dt_bias[0], ssm_a_log[0], ssm_d[0], ssm_norm_w[0], batch, seqlen,
                      cast=tuple(w.reshape(-1, w.shape[-1]) for w in ffn_stacks))
    ffn_weights = tuple(w_bf.reshape(w.shape) for w_bf, w in zip(ffn_bf, ffn_stacks))

    def block(v, layer, idx, **kw):
        return _block(v, norm_w[layer, 2 * idx], ffn_weights, (layer, idx), seqlen=seqlen, **kw)

    xt, k, v = block(xt, 0, 1, pre=(yn, bf(ssm_w_out[0]), None),
                     post=("kv", kv_norm_w, bf(w_k), b_k, bf(w_v), b_v, cos, sin))

    xt, q = block(xt, 1, 0, post=("q", norm_w[1, 1], bf(attn_w_q[0]), attn_b_q[0], cos, sin))
    (xt,) = block(xt, 1, 1, pre=("attn", attn_sinks[0], q, k, v, bf(attn_w_o[0]), attn_b_o[0]),
                  final_nw=final_norm_w)
    return xt.reshape(batch, seqlen, d)
```

```python
import functools
import math

import jax
import jax.numpy as jnp
from jax import lax
from jax.experimental import pallas as pl
from jax.experimental.pallas import tpu as pltpu

F32 = jnp.float32
BF16 = jnp.bfloat16

SSM_HEAD_DIM = 64
SSM_GROUPS = 4
SSM_STATE = 128
CONV_WIDTH = 4
CHUNK = 128
ATT_HEAD_DIM = 64
N_KV_HEADS = 4
WINDOW = 128
ROPE_THETA = 10000.0
FFN_RES_WEIGHT = 0.5
EPS = 1e-5
LOG2_E = 1.0 / math.log(2.0)

LANES = 128
SUBLANES = 8
VMEM_LIMIT_BYTES = 56 * 1024 * 1024

TOKEN_TILE = 512
BLOCK_SUBTILES = 2
CONV_COL_BLOCK = 512
CONV_ROW_STRIP = 32
SEQS_PER_STEP = 2


def _rmsnorm(x, w):
    return x * lax.rsqrt(jnp.mean(x * x, axis=-1, keepdims=True) + EPS) * w


def _silu(x):
    return x / (1.0 + jnp.exp2(x * -LOG2_E))


def _softplus(x):
    return jnp.maximum(x, 0.0) + jnp.log(1.0 + jnp.exp(-jnp.abs(x)))


def _resident(shape):
    zeros = (0,) * len(shape)
    return pl.BlockSpec(shape, lambda *_: zeros, pipeline_mode=pl.Buffered(1))


def _params(semantics):
    return pltpu.CompilerParams(dimension_semantics=semantics, vmem_limit_bytes=VMEM_LIMIT_BYTES)


def _rope_tables(seqlen):
    pos = jnp.arange(seqlen, dtype=F32)
    inv = 1.0 / (ROPE_THETA ** (jnp.arange(0, ATT_HEAD_DIM, 2, dtype=F32) / ATT_HEAD_DIM))
    ang = pos[:, None] * inv[None, :]
    cos, sin = jnp.cos(ang), jnp.sin(ang)
    reps = LANES // ATT_HEAD_DIM
    return (jnp.tile(jnp.concatenate([cos, cos], axis=-1), (1, reps)),
            jnp.tile(jnp.concatenate([-sin, sin], axis=-1), (1, reps)))


def _lane_blocks(t):
    return [t[:, j * LANES:(j + 1) * LANES] for j in range(t.shape[1] // LANES)]


def _rope(t, cos, sin_signed):
    half = ATT_HEAD_DIM // 2
    lane = lax.broadcasted_iota(jnp.int32, (t.shape[0], LANES), 1)
    first_half = (lane % ATT_HEAD_DIM) < half
    out = []
    for blk in _lane_blocks(t):
        partner = jnp.where(first_half, pltpu.roll(blk, LANES - half, axis=1), pltpu.roll(blk, half, axis=1))
        out.append(blk * cos + partner * sin_signed)
    return jnp.concatenate(out, axis=1)


def _separate_heads(t):
    low_half = lax.broadcasted_iota(jnp.int32, (t.shape[0], LANES), 1) < ATT_HEAD_DIM
    out = []
    for blk in _lane_blocks(t):
        swapped = pltpu.roll(blk, ATT_HEAD_DIM, axis=1)
        out += [jnp.where(low_half, blk, 0.0), jnp.where(low_half, 0.0, swapped),
                jnp.where(low_half, swapped, 0.0), jnp.where(low_half, 0.0, blk)]
    return jnp.concatenate(out, axis=1)


def _block_body(*refs, has_pre, has_pre_bias, post, final_norm, scale, n_cast, attn):
    if attn is not None:
        *refs, att_ref = refs
    it = iter(refs)
    x_ref = next(it)
    if attn is not None:
        sink_ref, q_ref, kc_ref, kp_ref, vc_ref, vp_ref = (next(it) for _ in range(6))
        a_ref, wp_ref, bp_ref = att_ref, next(it), next(it) if has_pre_bias else None
    else:
        a_ref, wp_ref, bp_ref = (next(it), next(it), next(it) if has_pre_bias else None) if has_pre else (None,) * 3
    nw_ref, wg_ref, wu_ref, wd_ref = next(it), next(it), next(it), next(it)
    if post == "kv":
        pnw_ref, wk_ref, bk_ref, wv_ref, bv_ref, cos_ref, sin_ref = (next(it) for _ in range(7))
    elif post == "q":
        pnw_ref, wq_ref, bq_ref, cos_ref, sin_ref = (next(it) for _ in range(5))
    fnw_ref = next(it) if final_norm else None
    cast_in = [next(it) for _ in range(n_cast)]
    o_ref = next(it)
    outs = list(it)
    post_out, cast_out = outs[:len(outs) - n_cast], outs[len(outs) - n_cast:]
    for src, dst in zip(cast_in, cast_out):
        dst[...] = src[...].astype(BF16)

    sub = x_ref.shape[0] // BLOCK_SUBTILES
    dot = lambda a, w_ref: jnp.dot(a, w_ref[...], preferred_element_type=F32)

    def load(s, rows):
        s["x"] = x_ref[rows]
        if has_pre:
            s["x"] = s["x"] + dot(a_ref[rows], wp_ref)
            if has_pre_bias:
                s["x"] = s["x"] + bp_ref[...]

    def norm(s, rows):
        s["h"] = _rmsnorm(s["x"], nw_ref[...]).astype(BF16)

    def gate_up(s, rows):
        s["g"] = dot(s["h"], wg_ref)
        s["u"] = dot(s["h"], wu_ref)

    def activate(s, rows):
        s["a"] = (_silu(s["g"]) * s["u"]).astype(BF16)

    def down(s, rows):
        s["y"] = s["x"] + FFN_RES_WEIGHT * dot(s["a"], wd_ref)

    def post_norm(s, rows):
        s["hp"] = _rmsnorm(s["y"], pnw_ref[...]).astype(BF16)

    def post_project(s, rows):
        if post == "kv":
            s["k"] = dot(s["hp"], wk_ref) + bk_ref[...]
            s["v"] = dot(s["hp"], wv_ref) + bv_ref[...]
        else:
            s["q"] = dot(s["hp"], wq_ref) + bq_ref[...]

    def store(s, rows):
        o_ref[rows] = _rmsnorm(s["y"], fnw_ref[...]) if final_norm else s["y"]
        if post == "kv":
            post_out[0][rows] = _separate_heads(_rope(s["k"], cos_ref[rows], sin_ref[rows])).astype(BF16)
            post_out[1][rows] = _separate_heads(s["v"]).astype(BF16)
        elif post == "q":
            post_out[0][rows] = (_rope(s["q"], cos_ref[rows], sin_ref[rows]) * scale).astype(BF16)

    states = [{} for _ in range(BLOCK_SUBTILES)]
    run = lambda stage, i: stage(states[i], slice(i * sub, (i + 1) * sub))
    every = lambda stage: [run(stage, i) for i in range(BLOCK_SUBTILES)]

    if attn is None:
        for stage in [load, norm, gate_up, activate, down] + ([post_norm, post_project] if post else []) + [store]:
            every(stage)
        return

    n_tiles, tiles_per_seq = attn
    step = pl.program_id(0)

    @pl.when(step == 0)
    def _():
        att_ref[...] = jnp.zeros_like(att_ref)

    has_prev = jnp.minimum(step, n_tiles - 1) % tiles_per_seq > 0
    q_blocks = list(range(q_ref.shape[0] // WINDOW))
    halves = [q_blocks[:len(q_blocks) // 2], q_blocks[len(q_blocks) // 2:]]
    every(load)
    scores = _attn_scores(q_ref, kc_ref, kp_ref, halves[0])
    every(norm)
    run(gate_up, 0)
    _attn_outputs(scores, sink_ref, vc_ref, vp_ref, has_prev, att_ref)
    scores = _attn_scores(q_ref, kc_ref, kp_ref, halves[1])
    for i in range(1, BLOCK_SUBTILES):
        run(gate_up, i)
    _attn_outputs(scores, sink_ref, vc_ref, vp_ref, has_prev, att_ref)
    for stage in [activate, down, store]:
        every(stage)


def _block(x, nw, ffn_weights, ffn_index, *, pre=None, post=None, final_nw=None, seqlen=None, cast=(),
           tile=TOKEN_TILE):
    t, d = x.shape
    tm = tile
    n_tiles = t // tm
    attn = pre is not None and pre[0] == "attn"
    row = lambda w: pl.BlockSpec((tm, w), (lambda i: (jnp.maximum(i - 1, 0), 0)) if attn else (lambda i: (i, 0)))
    vec = lambda v: v.reshape(1, -1)
    args, in_specs = [x], [row(d)]

    def add_resident(*arrays):
        for arr in arrays:
            args.append(arr)
            in_specs.append(_resident(arr.shape))

    if attn:
        _, sinks, q, k, v, wp, bp = pre
        tiles_per_seq = seqlen // tm
        tile = lambda i: jnp.minimum(i, n_tiles - 1)
        cur = lambda w: pl.BlockSpec((tm, w), lambda i: (tile(i), 0))
        prev = lambda w: pl.BlockSpec(
            (WINDOW, w), lambda i: (tile(i) * (tm // WINDOW) - jnp.where(tile(i) % tiles_per_seq > 0, 1, 0), 0))
        args += [sinks, q, k, k, v, v]
        in_specs += [pl.BlockSpec(memory_space=pltpu.SMEM), cur(q.shape[1]), cur(k.shape[1]), prev(k.shape[1]),
                     cur(v.shape[1]), prev(v.shape[1])]
        add_resident(wp, vec(bp))
    elif pre is not None:
        a, wp, bp = pre
        args.append(a)
        in_specs.append(row(a.shape[1]))
        add_resident(wp)
        if bp is not None:
            add_resident(vec(bp))
    add_resident(vec(nw))
    n_lead = len(ffn_index)
    for w in ffn_weights:
        args.append(w)
        in_specs.append(pl.BlockSpec((None,) * n_lead + w.shape[n_lead:],
                                     lambda *_: tuple(ffn_index) + (0,) * (w.ndim - n_lead),
                                     pipeline_mode=pl.Buffered(1)))
    out_specs = [row(d)]
    out_shape = [jax.ShapeDtypeStruct((t, d), F32)]
    kind = None
    if post is not None:
        kind = post[0]
        table = pl.BlockSpec((tm, LANES), lambda i: (i % (seqlen // tm), 0))
        if kind == "kv":
            _, pnw, wk, bk, wv, bv, cos, sin = post
            add_resident(vec(pnw), wk, vec(bk), wv, vec(bv))
            sep = 2 * (LANES // ATT_HEAD_DIM) * wk.shape[1]
            out_specs += [row(sep), row(sep)]
            out_shape += [jax.ShapeDtypeStruct((t, sep), BF16)] * 2
        else:
            _, pnw, wq, bq, cos, sin = post
            add_resident(vec(pnw), wq, vec(bq))
            out_specs.append(row(wq.shape[1]))
            out_shape.append(jax.ShapeDtypeStruct((t, wq.shape[1]), BF16))
        args += [cos, sin]
        in_specs += [table, table]
    if final_nw is not None:
        add_resident(vec(final_nw))
    cast_specs, cast_shapes = _cast_specs(cast, t // tm, lambda i: i)
    args += list(cast)
    in_specs += cast_specs
    out_specs += cast_specs
    out_shape += cast_shapes
    body = functools.partial(_block_body, has_pre=pre is not None, has_pre_bias=pre is not None and pre[-1] is not None,
                             post=kind, final_norm=final_nw is not None, scale=LOG2_E / math.sqrt(ATT_HEAD_DIM),
                             n_cast=len(cast), attn=(n_tiles, tiles_per_seq) if attn else None)
    return pl.pallas_call(
        body,
        grid=(n_tiles + 1 if attn else n_tiles,),
        in_specs=in_specs,
        out_specs=out_specs,
        out_shape=out_shape,
        scratch_shapes=[pltpu.VMEM((tm, q.shape[1]), BF16)] if attn else [],
        compiler_params=_params(("arbitrary",) if attn else ("parallel",)),
        name="block",
    )(*args)


def _ssm_in_body(x_ref, nw_ref, w_ref, wdt_ref, cw_ref, cb_ref, *rest, tiles_per_seq, d_inner, gn):
    n_cast = (len(rest) - 6) // 2
    cast_in, (gate_ref, xs_ref, b_ref, c_ref, dt_ref) = rest[:n_cast], rest[n_cast:n_cast + 5]
    cast_out, u_ref = rest[n_cast + 5:-1], rest[-1]
    for src, dst in zip(cast_in, cast_out):
        dst[...] = src[...].astype(BF16)
    tm = x_ref.shape[0]
    conv_dim = d_inner + 2 * gn
    cblk = CONV_COL_BLOCK
    n_blk = conv_dim // cblk

    @pl.when(pl.program_id(0) % tiles_per_seq == 0)
    def _():
        u_ref[:, :SUBLANES, :] = jnp.zeros((u_ref.shape[0], SUBLANES, LANES), F32)

    h = _rmsnorm(x_ref[...], nw_ref[...]).astype(BF16)

    def project(blk):
        return jnp.dot(h, w_ref[:, d_inner + blk * cblk:d_inner + (blk + 1) * cblk], preferred_element_type=F32)

    def conv(u, blk):
        lo = blk * cblk
        for s in range(cblk // LANES):
            slab = blk * (cblk // LANES) + s
            cs = slice(lo + s * LANES, lo + (s + 1) * LANES)
            u_ref[slab, SUBLANES:, :] = u[:, s * LANES:(s + 1) * LANES]
            for r in range(0, tm, CONV_ROW_STRIP):
                rows = slice(r, r + CONV_ROW_STRIP)
                acc = cb_ref[:, cs]
                for k in range(CONV_WIDTH):
                    shift = CONV_WIDTH - 1 - k
                    acc = acc + cw_ref[k:k + 1, cs] * u_ref[slab, pl.ds(SUBLANES - shift + r, CONV_ROW_STRIP), :]
                act = _silu(acc)
                if lo < d_inner:
                    xs_ref[rows, cs] = act
                elif lo < d_inner + gn:
                    b_ref[rows, lo - d_inner + s * LANES:lo - d_inner + (s + 1) * LANES] = act.astype(BF16)
                else:
                    off = lo - d_inner - gn
                    c_ref[rows, off + s * LANES:off + (s + 1) * LANES] = act.astype(BF16)
            u_ref[slab, :SUBLANES, :] = u_ref[slab, tm:, :]

    u_next = project(0)
    for blk in range(n_blk):
        u = u_next
        if blk + 1 < n_blk:
            u_next = project(blk + 1)
        else:
            gate_ref[...] = _silu(jnp.dot(h, w_ref[:, :d_inner], preferred_element_type=F32))
            dt_ref[...] = jnp.dot(h, wdt_ref[...], preferred_element_type=F32)
        conv(u, blk)


def _ssm_in(x, nw, w_in, wdt, conv_w, conv_b, d_inner, seqlen, cast=()):
    t, d = x.shape
    conv_dim = conv_w.shape[1]
    gn = (conv_dim - d_inner) // 2
    assert d_inner % CONV_COL_BLOCK == 0 and gn % CONV_COL_BLOCK == 0
    tm = TOKEN_TILE
    row = lambda n: pl.BlockSpec((tm, n), lambda i: (i, 0))
    cast_specs, cast_shapes = _cast_specs(cast, t // tm, lambda i: i)
    outs = pl.pallas_call(
        functools.partial(_ssm_in_body, tiles_per_seq=seqlen // tm, d_inner=d_inner, gn=gn),
        grid=(t // tm,),
        in_specs=[row(d), _resident((1, d)), _resident(w_in.shape),
                  _resident((d, LANES)), _resident((CONV_WIDTH, conv_dim)), _resident((1, conv_dim))] + cast_specs,
        out_specs=[row(d_inner), row(d_inner), row(gn), row(gn), row(LANES)] + cast_specs,
        out_shape=[jax.ShapeDtypeStruct((t, d_inner), F32), jax.ShapeDtypeStruct((t, d_inner), F32),
                   jax.ShapeDtypeStruct((t, gn), BF16), jax.ShapeDtypeStruct((t, gn), BF16),
                   jax.ShapeDtypeStruct((t, LANES), F32)] + cast_shapes,
        scratch_shapes=[pltpu.VMEM((conv_dim // LANES, SUBLANES + tm, LANES), F32)],
        compiler_params=_params(("arbitrary",)),
        name="ssm_in",
    )(x, nw.reshape(1, d), w_in, wdt, conv_w, conv_b.reshape(1, conv_dim), *cast)
    return outs[:5], outs[5:]


def _cumsum_rows(x):
    n = x.shape[0]
    row = lax.broadcasted_iota(jnp.int32, x.shape, 0)
    step = 1
    while step < n:
        x = x + jnp.where(row >= step, pltpu.roll(x, step, axis=0), 0.0)
        step *= 2
    return x


def _expand_heads(v, e, heads):
    valid = lax.broadcasted_iota(jnp.int32, v.shape, 1) < heads
    hi = v.astype(BF16).astype(F32)
    r1 = v - hi
    mid = r1.astype(BF16).astype(F32)
    lo = (r1 - mid).astype(BF16).astype(F32)
    keep = lambda p: jnp.where(valid, p, 0.0)
    packed = keep(hi) + pltpu.roll(keep(mid), heads, axis=1) + pltpu.roll(keep(lo), 2 * heads, axis=1)
    return jnp.dot(packed.astype(BF16), e, preferred_element_type=F32)


def _cast_specs(arrays, n_steps, step_of):
    specs, shapes = [], []
    for arr in arrays:
        rows = arr.shape[0] // n_steps
        assert rows * n_steps == arr.shape[0] and rows % (2 * SUBLANES) == 0
        specs.append(pl.BlockSpec((rows, arr.shape[1]), lambda *idx: (step_of(*idx), 0)))
        shapes.append(jax.ShapeDtypeStruct(arr.shape, BF16))
    return specs, shapes


def _ssd_body(gate_ref, xs_ref, b_ref, c_ref, dt_ref, dtb_ref, alog_ref, dskip_ref, nw_ref, e_ref, o_ref, state_ref):
    L = CHUNK
    n_state = SSM_STATE
    pair = 2 * SSM_HEAD_DIM
    d_inner = xs_ref.shape[-1]
    heads = d_inner // SSM_HEAD_DIM
    n_pairs = d_inner // pair
    pairs_per_group = n_pairs // SSM_GROUPS
    group_w = d_inner // SSM_GROUPS
    seqs = range(SEQS_PER_STEP)

    @pl.when(pl.program_id(1) == 0)
    def _():
        state_ref[...] = jnp.zeros_like(state_ref)

    row = lax.broadcasted_iota(jnp.int32, (L, L), 0)
    col = lax.broadcasted_iota(jnp.int32, (L, L), 1)
    causal = col <= row
    low_half = lax.broadcasted_iota(jnp.int32, (L, pair), 1) < SSM_HEAD_DIM
    group = lambda ref, seq, g: ref[seq, :, g * n_state:(g + 1) * n_state]

    def prepare(seq):
        dt = _softplus(dt_ref[seq] + dtb_ref[...])
        acs = _cumsum_rows(dt * (-jnp.exp(alog_ref[...]) * LOG2_E))
        acs_t = acs.T
        dt_t = dt.T
        cb, b_t = [], []
        for g in range(SSM_GROUPS):
            bg = group(b_ref, seq, g)
            cb.append(lax.dot_general(group(c_ref, seq, g), bg, (((1,), (1,)), ((), ())),
                                      preferred_element_type=F32))
            b_t.append(bg.astype(F32).T)
        return dict(
            acs=acs, cb=cb, b_t=b_t,
            e_out=_expand_heads(jnp.exp2(acs), e_ref[...], heads),
            w_state_t=jnp.exp2(acs_t[:, L - 1:L] - acs_t) * dt_t,
            src_t=acs_t - jnp.log2(dt_t))

    def head_pair(seq, p, j):
        g = j // pairs_per_group
        m_blocks, bt_blocks = [], []
        for h in (2 * j, 2 * j + 1):
            seg = p["acs"][:, h:h + 1] - p["src_t"][h:h + 1, :]
            m_blocks.append((p["cb"][g] * jnp.exp2(jnp.where(causal, seg, -jnp.inf))).astype(BF16))
            bt_blocks.append((p["b_t"][g] * p["w_state_t"][h:h + 1, :]).astype(BF16))
        lhs = jnp.concatenate([jnp.concatenate(m_blocks, axis=1), jnp.concatenate(bt_blocks, axis=1)], axis=0)
        xb = xs_ref[seq, :, j * pair:(j + 1) * pair]
        rhs = jnp.concatenate([jnp.where(low_half, xb, 0.0), jnp.where(low_half, 0.0, xb)], axis=0).astype(BF16)
        res = jnp.dot(lhs, rhs, preferred_element_type=F32)
        return res[:L], res[L:]

    def finish(seq, p, y_diag, d_state):
        state = state_ref[seq]
        y_off = [jnp.dot(group(c_ref, seq, g), state[:, g * group_w:(g + 1) * group_w].astype(BF16),
                         preferred_element_type=F32) for g in range(SSM_GROUPS)]
        state_ref[seq] = state * p["e_out"][L - 1:L, :] + jnp.concatenate(d_state, axis=1)
        y = (jnp.concatenate(y_diag, axis=1) + jnp.concatenate(y_off, axis=1) * p["e_out"]
             + xs_ref[seq] * dskip_ref[...])
        y = y * gate_ref[seq]
        normed = []
        for g in range(SSM_GROUPS):
            yg = y[:, g * group_w:(g + 1) * group_w]
            normed.append(yg * lax.rsqrt(jnp.mean(yg * yg, axis=-1, keepdims=True) + EPS))
        o_ref[seq] = (jnp.concatenate(normed, axis=1) * nw_ref[...]).astype(BF16)

    prepared = [prepare(seq) for seq in seqs]
    results = [([], []) for _ in seqs]
    for j in range(n_pairs):
        for seq in seqs:
            y_j, s_j = head_pair(seq, prepared[seq], j)
            results[seq][0].append(y_j)
            results[seq][1].append(s_j)
    for seq in seqs:
        finish(seq, prepared[seq], *results[seq])


def _ssd(gate, xs, b, c, dt_raw, dt_bias, a_log, d_skip, norm_w, batch, seqlen):
    t, d_inner = xs.shape
    gn = b.shape[1]
    heads = d_inner // SSM_HEAD_DIM
    assert 3 * heads <= LANES and batch % SEQS_PER_STEP == 0
    nc = seqlen // CHUNK
    pad = lambda v: jnp.pad(v.reshape(1, heads), ((0, 0), (0, LANES - heads)))
    piece_head = jnp.where(jnp.arange(LANES) < 3 * heads, jnp.arange(LANES) % heads, -1)
    expand = (piece_head[:, None] == (jnp.arange(d_inner) // SSM_HEAD_DIM)[None, :]).astype(BF16)
    seq3 = lambda v: v.reshape(batch, seqlen, v.shape[1])
    blk = lambda n: pl.BlockSpec((SEQS_PER_STEP, CHUNK, n), lambda bi, ci: (bi, ci, 0))
    out = pl.pallas_call(
        _ssd_body,
        grid=(batch // SEQS_PER_STEP, nc),
        in_specs=[blk(d_inner), blk(d_inner), blk(gn), blk(gn), blk(LANES),
                  _resident((1, LANES)), _resident((1, LANES)), _resident((1, d_inner)),
                  _resident((1, d_inner)), _resident((LANES, d_inner))],
        out_specs=blk(d_inner),
        out_shape=jax.ShapeDtypeStruct((batch, seqlen, d_inner), BF16),
        scratch_shapes=[pltpu.VMEM((SEQS_PER_STEP, SSM_STATE, d_inner), F32)],
        compiler_params=_params(("arbitrary", "arbitrary")),
        name="ssd",
    )(seq3(gate), seq3(xs), seq3(b), seq3(c), seq3(dt_raw), pad(dt_bias), pad(a_log),
      jnp.repeat(d_skip, SSM_HEAD_DIM).reshape(1, d_inner), norm_w.reshape(1, d_inner), expand)
    return out.reshape(t, d_inner)


def _key_band(cur_ref, prev_ref, qi, kh):
    W = WINDOW
    parts = []
    for half in range(2):
        cs = slice((2 * kh + half) * LANES, (2 * kh + half + 1) * LANES)
        parts.append(prev_ref[:, cs] if qi == 0 else cur_ref[(qi - 1) * W:qi * W, cs])
        parts.append(cur_ref[qi * W:(qi + 1) * W, cs])
    return jnp.concatenate(parts, axis=0)


def _attn_scores(q_ref, kc_ref, kp_ref, q_blocks):
    W = WINDOW
    blocks_per_kv = q_ref.shape[1] // LANES // N_KV_HEADS
    scores = {}
    for qi in q_blocks:
        rows = slice(qi * W, (qi + 1) * W)
        for kh in range(N_KV_HEADS):
            q_rows = jnp.concatenate([q_ref[rows, (kh * blocks_per_kv + jj) * LANES:(kh * blocks_per_kv + jj + 1) * LANES]
                                      for jj in range(blocks_per_kv)], axis=0)
            scores[qi, kh] = lax.dot_general(q_rows, _key_band(kc_ref, kp_ref, qi, kh), (((1,), (1,)), ((), ())),
                                             preferred_element_type=F32)
    return scores


def _attn_outputs(scores, sink_ref, vc_ref, vp_ref, has_prev, o_ref):
    W = WINDOW
    blocks_per_kv = o_ref.shape[1] // LANES // N_KV_HEADS
    row = lax.broadcasted_iota(jnp.int32, (W, 2 * W), 0)
    col = lax.broadcasted_iota(jnp.int32, (W, 2 * W), 1)
    in_window = (col <= row + W) & (col > row)
    low_half = lax.broadcasted_iota(jnp.int32, (W, LANES), 1) < ATT_HEAD_DIM
    for qi in sorted({qi for qi, _ in scores}):
        rows = slice(qi * W, (qi + 1) * W)
        mask = in_window
        if qi == 0:
            mask = mask & (col >= jnp.where(has_prev, 0, W))
        for kh in range(N_KV_HEADS):
            s = scores[qi, kh]
            weights, inv_sum = [], []
            for jj in range(blocks_per_kv):
                halves, inv = [], []
                for t in range(2):
                    sink = sink_ref[2 * (kh * blocks_per_kv + jj) + t] * LOG2_E
                    sh = jnp.where(mask, s[jj * W:(jj + 1) * W, t * 2 * W:(t + 1) * 2 * W], -jnp.inf)
                    m = jnp.maximum(jnp.max(sh, axis=-1, keepdims=True), sink)
                    p = jnp.exp2(sh - m)
                    inv.append(1.0 / (jnp.sum(p, axis=-1, keepdims=True) + jnp.exp2(sink - m)))
                    halves.append(p.astype(BF16))
                weights.append(jnp.concatenate(halves, axis=1))
                inv_sum.append(jnp.where(low_half, inv[0], inv[1]))
            o = jnp.dot(jnp.concatenate(weights, axis=0), _key_band(vc_ref, vp_ref, qi, kh),
                        preferred_element_type=F32)
            for jj in range(blocks_per_kv):
                j = kh * blocks_per_kv + jj
                o_ref[rows, j * LANES:(j + 1) * LANES] = (o[jj * W:(jj + 1) * W] * inv_sum[jj]).astype(BF16)


def kernel(x, norm_w, ffn_w_gate, ffn_w_up, ffn_w_down, ssm_w_in, ssm_conv_w, ssm_conv_b, ssm_dt_bias, ssm_a_log, ssm_d, ssm_norm_w, ssm_w_out, kv_norm_w, w_k, b_k, w_v, b_v, attn_w_q, attn_b_q, attn_sinks, attn_w_o, attn_b_o, final_norm_w):
    batch, seqlen, d = x.shape
    assert seqlen % TOKEN_TILE == 0 and TOKEN_TILE % (2 * WINDOW) == 0 and seqlen % CHUNK == 0
    assert ssm_w_in.shape[0] == 1 and attn_w_q.shape[0] == 1 and norm_w.shape[0] == 2
    d_inner = ssm_w_out.shape[1]
    heads = d_inner // SSM_HEAD_DIM
    conv_dim = ssm_conv_w.shape[-1]
    bf = lambda w: w.astype(BF16)
    xt = x.reshape(batch * seqlen, d)
    cos, sin = _rope_tables(seqlen)

    ffn_stacks = (ffn_w_gate, ffn_w_up, ffn_w_down)

    (xt,) = _block(xt, norm_w[0, 0], tuple(bf(w[0, 0]) for w in ffn_stacks), (), tile=2 * TOKEN_TILE)
    w_in = bf(ssm_w_in[0])
    w_dt = jnp.pad(w_in[:, d_inner + conv_dim:], ((0, 0), (0, LANES - heads)))
    (gate, xs, b, c, dt_raw), ffn_bf = _ssm_in(xt, norm_w[0, 1], w_in, w_dt, ssm_conv_w[0], ssm_conv_b[0], d_inner,
                                               seqlen, cast=tuple(w.reshape(-1, w.shape[-1]) for w in ffn_stacks))
    yn = _ssd(gate, xs, b, c, dt_raw, ssm_dt_bias[0], ssm_a_log[0], ssm_d[0], ssm_norm_w[0], batch, seqlen)
    ffn_weights = tuple(w_bf.reshape(w.shape) for w_bf, w in zip(ffn_bf, ffn_stacks))

    def block(v, layer, idx, **kw):
        return _block(v, norm_w[layer, 2 * idx], ffn_weights, (layer, idx), seqlen=seqlen, **kw)

    xt, k, v = block(xt, 0, 1, pre=(yn, bf(ssm_w_out[0]), None),
                     post=("kv", kv_norm_w, bf(w_k), b_k, bf(w_v), b_v, cos, sin))

    xt, q = block(xt, 1, 0, post=("q", norm_w[1, 1], bf(attn_w_q[0]), attn_b_q[0], cos, sin))
    (xt,) = block(xt, 1, 1, pre=("attn", attn_sinks[0], q, k, v, bf(attn_w_o[0]), attn_b_o[0]),
                  final_nw=final_norm_w)
    return xt.reshape(batch, seqlen, d)
```

```python
import functools
import math

import jax
import jax.numpy as jnp
from jax import lax
from jax.experimental import pallas as pl
from jax.experimental.pallas import tpu as pltpu

F32 = jnp.float32
BF16 = jnp.bfloat16

SSM_HEAD_DIM = 64
SSM_GROUPS = 4
SSM_STATE = 128
CONV_WIDTH = 4
CHUNK = 128
ATT_HEAD_DIM = 64
N_KV_HEADS = 4
WINDOW = 128
ROPE_THETA = 10000.0
FFN_RES_WEIGHT = 0.5
EPS = 1e-5
LOG2_E = 1.0 / math.log(2.0)

LANES = 128
SUBLANES = 8
VMEM_LIMIT_BYTES = 56 * 1024 * 1024

TOKEN_TILE = 512
BLOCK_SUBTILES = 2
CONV_COL_BLOCK = 512
SEQS_PER_STEP = 2


def _rmsnorm(x, w):
    return x * lax.rsqrt(jnp.mean(x * x, axis=-1, keepdims=True) + EPS) * w


def _silu(x):
    return x / (1.0 + jnp.exp2(x * -LOG2_E))


def _softplus(x):
    return jnp.maximum(x, 0.0) + jnp.log(1.0 + jnp.exp(-jnp.abs(x)))


def _resident(shape):
    zeros = (0,) * len(shape)
    return pl.BlockSpec(shape, lambda *_: zeros, pipeline_mode=pl.Buffered(1))


def _params(semantics):
    return pltpu.CompilerParams(dimension_semantics=semantics, vmem_limit_bytes=VMEM_LIMIT_BYTES)


def _rope_tables(seqlen):
    pos = jnp.arange(seqlen, dtype=F32)
    inv = 1.0 / (ROPE_THETA ** (jnp.arange(0, ATT_HEAD_DIM, 2, dtype=F32) / ATT_HEAD_DIM))
    ang = pos[:, None] * inv[None, :]
    cos, sin = jnp.cos(ang), jnp.sin(ang)
    reps = LANES // ATT_HEAD_DIM
    return (jnp.tile(jnp.concatenate([cos, cos], axis=-1), (1, reps)),
            jnp.tile(jnp.concatenate([-sin, sin], axis=-1), (1, reps)))


def _lane_blocks(t):
    return [t[:, j * LANES:(j + 1) * LANES] for j in range(t.shape[1] // LANES)]


def _rope(t, cos, sin_signed):
    half = ATT_HEAD_DIM // 2
    lane = lax.broadcasted_iota(jnp.int32, (t.shape[0], LANES), 1)
    first_half = (lane % ATT_HEAD_DIM) < half
    out = []
    for blk in _lane_blocks(t):
        partner = jnp.where(first_half, pltpu.roll(blk, LANES - half, axis=1), pltpu.roll(blk, half, axis=1))
        out.append(blk * cos + partner * sin_signed)
    return jnp.concatenate(out, axis=1)


def _separate_heads(t):
    low_half = lax.broadcasted_iota(jnp.int32, (t.shape[0], LANES), 1) < ATT_HEAD_DIM
    out = []
    for blk in _lane_blocks(t):
        swapped = pltpu.roll(blk, ATT_HEAD_DIM, axis=1)
        out += [jnp.where(low_half, blk, 0.0), jnp.where(low_half, 0.0, swapped),
                jnp.where(low_half, swapped, 0.0), jnp.where(low_half, 0.0, blk)]
    return jnp.concatenate(out, axis=1)


def _block_body(*refs, has_pre, has_pre_bias, post, final_norm, scale, n_cast, attn):
    if attn is not None:
        *refs, att_ref = refs
    it = iter(refs)
    x_ref = next(it)
    if attn is not None:
        sink_ref, q_ref, kc_ref, kp_ref, vc_ref, vp_ref = (next(it) for _ in range(6))
        a_ref, wp_ref, bp_ref = att_ref, next(it), next(it) if has_pre_bias else None
    else:
        a_ref, wp_ref, bp_ref = (next(it), next(it), next(it) if has_pre_bias else None) if has_pre else (None,) * 3
    nw_ref, wg_ref, wu_ref, wd_ref = next(it), next(it), next(it), next(it)
    if post == "kv":
        pnw_ref, wk_ref, bk_ref, wv_ref, bv_ref, cos_ref, sin_ref = (next(it) for _ in range(7))
    elif post == "q":
        pnw_ref, wq_ref, bq_ref, cos_ref, sin_ref = (next(it) for _ in range(5))
    fnw_ref = next(it) if final_norm else None
    cast_in = [next(it) for _ in range(n_cast)]
    o_ref = next(it)
    outs = list(it)
    post_out, cast_out = outs[:len(outs) - n_cast], outs[len(outs) - n_cast:]
    for src, dst in zip(cast_in, cast_out):
        dst[...] = src[...].astype(BF16)

    sub = x_ref.shape[0] // BLOCK_SUBTILES
    dot = lambda a, w_ref: jnp.dot(a, w_ref[...], preferred_element_type=F32)

    def load(s, rows):
        s["x"] = x_ref[rows]
        if has_pre:
            s["x"] = s["x"] + dot(a_ref[rows], wp_ref)
            if has_pre_bias:
                s["x"] = s["x"] + bp_ref[...]

    def norm(s, rows):
        s["h"] = _rmsnorm(s["x"], nw_ref[...]).astype(BF16)

    def gate_up(s, rows):
        s["g"] = dot(s["h"], wg_ref)
        s["u"] = dot(s["h"], wu_ref)

    def activate(s, rows):
        s["a"] = (_silu(s["g"]) * s["u"]).astype(BF16)

    def down(s, rows):
        s["y"] = s["x"] + FFN_RES_WEIGHT * dot(s["a"], wd_ref)

    def post_norm(s, rows):
        s["hp"] = _rmsnorm(s["y"], pnw_ref[...]).astype(BF16)

    def post_project(s, rows):
        if post == "kv":
            s["k"] = dot(s["hp"], wk_ref) + bk_ref[...]
            s["v"] = dot(s["hp"], wv_ref) + bv_ref[...]
        else:
            s["q"] = dot(s["hp"], wq_ref) + bq_ref[...]

    def store(s, rows):
        o_ref[rows] = _rmsnorm(s["y"], fnw_ref[...]) if final_norm else s["y"]
        if post == "kv":
            post_out[0][rows] = _separate_heads(_rope(s["k"], cos_ref[rows], sin_ref[rows])).astype(BF16)
            post_out[1][rows] = _separate_heads(s["v"]).astype(BF16)
        elif post == "q":
            post_out[0][rows] = (_rope(s["q"], cos_ref[rows], sin_ref[rows]) * scale).astype(BF16)

    def block_stages(between=None):
        states = [{} for _ in range(BLOCK_SUBTILES)]
        for stage in [load, norm, gate_up, activate, down] + ([post_norm, post_project] if post else []) + [store]:
            for i, s in enumerate(states):
                stage(s, slice(i * sub, (i + 1) * sub))
                if between is not None:
                    between(stage, i)

    if attn is None:
        block_stages()
        return

    n_tiles, tiles_per_seq = attn
    step = pl.program_id(0)
    has_prev = step % tiles_per_seq > 0
    q_blocks = list(range(q_ref.shape[0] // WINDOW))
    halves = [q_blocks[:len(q_blocks) // 2], q_blocks[len(q_blocks) // 2:]]

    def attention(half):
        scores = _attn_scores(q_ref, kc_ref, kp_ref, half)
        _attn_outputs(scores, sink_ref, vc_ref, vp_ref, has_prev, att_ref)

    @pl.when(step == 0)
    def _():
        o_ref[...] = jnp.zeros_like(o_ref)
        for half in halves:
            attention(half)

    @pl.when((step > 0) & (step < n_tiles))
    def _():
        pending = {}

        def between(stage, i):
            if stage is load and i == BLOCK_SUBTILES - 1:
                pending[0] = _attn_scores(q_ref, kc_ref, kp_ref, halves[0])
            elif stage is gate_up and i == 0:
                _attn_outputs(pending.pop(0), sink_ref, vc_ref, vp_ref, has_prev, att_ref)
                pending[1] = _attn_scores(q_ref, kc_ref, kp_ref, halves[1])
            elif stage is gate_up and i == BLOCK_SUBTILES - 1:
                _attn_outputs(pending.pop(1), sink_ref, vc_ref, vp_ref, has_prev, att_ref)

        block_stages(between)

    @pl.when(step == n_tiles)
    def _():
        block_stages()


def _block(x, nw, ffn_weights, ffn_index, *, pre=None, post=None, final_nw=None, seqlen=None, cast=()):
    t, d = x.shape
    tm = TOKEN_TILE
    n_tiles = t // tm
    attn = pre is not None and pre[0] == "attn"
    row = lambda w: pl.BlockSpec((tm, w), (lambda i: (jnp.maximum(i - 1, 0), 0)) if attn else (lambda i: (i, 0)))
    vec = lambda v: v.reshape(1, -1)
    args, in_specs = [x], [row(d)]

    def add_resident(*arrays):
        for arr in arrays:
            args.append(arr)
            in_specs.append(_resident(arr.shape))

    if attn:
        _, sinks, q, k, v, wp, bp = pre
        tiles_per_seq = seqlen // tm
        tile = lambda i: jnp.minimum(i, n_tiles - 1)
        cur = lambda w: pl.BlockSpec((tm, w), lambda i: (tile(i), 0))
        prev = lambda w: pl.BlockSpec(
            (WINDOW, w), lambda i: (tile(i) * (tm // WINDOW) - jnp.where(tile(i) % tiles_per_seq > 0, 1, 0), 0))
        args += [sinks, q, k, k, v, v]
        in_specs += [pl.BlockSpec(memory_space=pltpu.SMEM), cur(q.shape[1]), cur(k.shape[1]), prev(k.shape[1]),
                     cur(v.shape[1]), prev(v.shape[1])]
        add_resident(wp, vec(bp))
    elif pre is not None:
        a, wp, bp = pre
        args.append(a)
        in_specs.append(row(a.shape[1]))
        add_resident(wp)
        if bp is not None:
            add_resident(vec(bp))
    add_resident(vec(nw))
    n_lead = len(ffn_index)
    for w in ffn_weights:
        args.append(w)
        in_specs.append(pl.BlockSpec((None,) * n_lead + w.shape[n_lead:],
                                     lambda *_: tuple(ffn_index) + (0,) * (w.ndim - n_lead),
                                     pipeline_mode=pl.Buffered(1)))
    out_specs = [row(d)]
    out_shape = [jax.ShapeDtypeStruct((t, d), F32)]
    kind = None
    if post is not None:
        kind = post[0]
        table = pl.BlockSpec((tm, LANES), lambda i: (i % (seqlen // tm), 0))
        if kind == "kv":
            _, pnw, wk, bk, wv, bv, cos, sin = post
            add_resident(vec(pnw), wk, vec(bk), wv, vec(bv))
            sep = 2 * (LANES // ATT_HEAD_DIM) * wk.shape[1]
            out_specs += [row(sep), row(sep)]
            out_shape += [jax.ShapeDtypeStruct((t, sep), BF16)] * 2
        else:
            _, pnw, wq, bq, cos, sin = post
            add_resident(vec(pnw), wq, vec(bq))
            out_specs.append(row(wq.shape[1]))
            out_shape.append(jax.ShapeDtypeStruct((t, wq.shape[1]), BF16))
        args += [cos, sin]
        in_specs += [table, table]
    if final_nw is not None:
        add_resident(vec(final_nw))
    cast_specs, cast_shapes = _cast_specs(cast, t // tm, lambda i: i)
    args += list(cast)
    in_specs += cast_specs
    out_specs += cast_specs
    out_shape += cast_shapes
    body = functools.partial(_block_body, has_pre=pre is not None, has_pre_bias=pre is not None and pre[-1] is not None,
                             post=kind, final_norm=final_nw is not None, scale=LOG2_E / math.sqrt(ATT_HEAD_DIM),
                             n_cast=len(cast), attn=(n_tiles, tiles_per_seq) if attn else None)
    return pl.pallas_call(
        body,
        grid=(n_tiles + 1 if attn else n_tiles,),
        in_specs=in_specs,
        out_specs=out_specs,
        out_shape=out_shape,
        scratch_shapes=[pltpu.VMEM((tm, q.shape[1]), BF16)] if attn else [],
        compiler_params=_params(("arbitrary",) if attn else ("parallel",)),
        name="block",
    )(*args)


def _ssm_in_body(x_ref, nw_ref, w_ref, wdt_ref, cw_ref, cb_ref, *rest, tiles_per_seq, d_inner, gn):
    n_cast = (len(rest) - 6) // 2
    cast_in, (gate_ref, xs_ref, b_ref, c_ref, dt_ref) = rest[:n_cast], rest[n_cast:n_cast + 5]
    cast_out, u_ref = rest[n_cast + 5:-1], rest[-1]
    for src, dst in zip(cast_in, cast_out):
        dst[...] = src[...].astype(BF16)
    tm = x_ref.shape[0]
    conv_dim = d_inner + 2 * gn
    cblk = CONV_COL_BLOCK
    n_blk = conv_dim // cblk

    @pl.when(pl.program_id(0) % tiles_per_seq == 0)
    def _():
        u_ref[:, :SUBLANES, :] = jnp.zeros((u_ref.shape[0], SUBLANES, LANES), F32)

    h = _rmsnorm(x_ref[...], nw_ref[...]).astype(BF16)

    def project(blk):
        return jnp.dot(h, w_ref[:, d_inner + blk * cblk:d_inner + (blk + 1) * cblk], preferred_element_type=F32)

    def conv(u, blk):
        lo = blk * cblk
        for s in range(cblk // LANES):
            slab = blk * (cblk // LANES) + s
            cs = slice(lo + s * LANES, lo + (s + 1) * LANES)
            u_ref[slab, SUBLANES:, :] = u[:, s * LANES:(s + 1) * LANES]
            acc = cb_ref[:, cs]
            for k in range(CONV_WIDTH):
                shift = CONV_WIDTH - 1 - k
                acc = acc + cw_ref[k:k + 1, cs] * u_ref[slab, pl.ds(SUBLANES - shift, tm), :]
            act = _silu(acc)
            if lo < d_inner:
                xs_ref[:, cs] = act
            elif lo < d_inner + gn:
                b_ref[:, lo - d_inner + s * LANES:lo - d_inner + (s + 1) * LANES] = act.astype(BF16)
            else:
                off = lo - d_inner - gn
                c_ref[:, off + s * LANES:off + (s + 1) * LANES] = act.astype(BF16)
            u_ref[slab, :SUBLANES, :] = u_ref[slab, tm:, :]

    u_next = project(0)
    for blk in range(n_blk):
        u = u_next
        if blk + 1 < n_blk:
            u_next = project(blk + 1)
        else:
            gate_ref[...] = _silu(jnp.dot(h, w_ref[:, :d_inner], preferred_element_type=F32))
            dt_ref[...] = jnp.dot(h, wdt_ref[...], preferred_element_type=F32)
        conv(u, blk)


def _ssm_in(x, nw, w_in, wdt, conv_w, conv_b, d_inner, seqlen, cast=()):
    t, d = x.shape
    conv_dim = conv_w.shape[1]
    gn = (conv_dim - d_inner) // 2
    assert d_inner % CONV_COL_BLOCK == 0 and gn % CONV_COL_BLOCK == 0
    tm = TOKEN_TILE
    row = lambda n: pl.BlockSpec((tm, n), lambda i: (i, 0))
    cast_specs, cast_shapes = _cast_specs(cast, t // tm, lambda i: i)
    outs = pl.pallas_call(
        functools.partial(_ssm_in_body, tiles_per_seq=seqlen // tm, d_inner=d_inner, gn=gn),
        grid=(t // tm,),
        in_specs=[row(d), _resident((1, d)), _resident(w_in.shape),
                  _resident((d, LANES)), _resident((CONV_WIDTH, conv_dim)), _resident((1, conv_dim))] + cast_specs,
        out_specs=[row(d_inner), row(d_inner), row(gn), row(gn), row(LANES)] + cast_specs,
        out_shape=[jax.ShapeDtypeStruct((t, d_inner), F32), jax.ShapeDtypeStruct((t, d_inner), F32),
                   jax.ShapeDtypeStruct((t, gn), BF16), jax.ShapeDtypeStruct((t, gn), BF16),
                   jax.ShapeDtypeStruct((t, LANES), F32)] + cast_shapes,
        scratch_shapes=[pltpu.VMEM((conv_dim // LANES, SUBLANES + tm, LANES), F32)],
        compiler_params=_params(("arbitrary",)),
        name="ssm_in",
    )(x, nw.reshape(1, d), w_in, wdt, conv_w, conv_b.reshape(1, conv_dim), *cast)
    return outs[:5], outs[5:]


def _cumsum_rows(x):
    n = x.shape[0]
    row = lax.broadcasted_iota(jnp.int32, x.shape, 0)
    step = 1
    while step < n:
        x = x + jnp.where(row >= step, pltpu.roll(x, step, axis=0), 0.0)
        step *= 2
    return x


def _expand_heads(v, e, heads):
    valid = lax.broadcasted_iota(jnp.int32, v.shape, 1) < heads
    hi = v.astype(BF16).astype(F32)
    r1 = v - hi
    mid = r1.astype(BF16).astype(F32)
    lo = (r1 - mid).astype(BF16).astype(F32)
    keep = lambda p: jnp.where(valid, p, 0.0)
    packed = keep(hi) + pltpu.roll(keep(mid), heads, axis=1) + pltpu.roll(keep(lo), 2 * heads, axis=1)
    return jnp.dot(packed.astype(BF16), e, preferred_element_type=F32)


def _cast_specs(arrays, n_steps, step_of):
    specs, shapes = [], []
    for arr in arrays:
        rows = arr.shape[0] // n_steps
        assert rows * n_steps == arr.shape[0] and rows % (2 * SUBLANES) == 0
        specs.append(pl.BlockSpec((rows, arr.shape[1]), lambda *idx: (step_of(*idx), 0)))
        shapes.append(jax.ShapeDtypeStruct(arr.shape, BF16))
    return specs, shapes


def _ssd_body(gate_ref, xs_ref, b_ref, c_ref, dt_ref, dtb_ref, alog_ref, dskip_ref, nw_ref, e_ref, o_ref, state_ref):
    L = CHUNK
    n_state = SSM_STATE
    pair = 2 * SSM_HEAD_DIM
    d_inner = xs_ref.shape[-1]
    heads = d_inner // SSM_HEAD_DIM
    n_pairs = d_inner // pair
    pairs_per_group = n_pairs // SSM_GROUPS
    group_w = d_inner // SSM_GROUPS
    seqs = range(SEQS_PER_STEP)

    @pl.when(pl.program_id(1) == 0)
    def _():
        state_ref[...] = jnp.zeros_like(state_ref)

    row = lax.broadcasted_iota(jnp.int32, (L, L), 0)
    col = lax.broadcasted_iota(jnp.int32, (L, L), 1)
    causal = col <= row
    low_half = lax.broadcasted_iota(jnp.int32, (L, pair), 1) < SSM_HEAD_DIM
    group = lambda ref, seq, g: ref[seq, :, g * n_state:(g + 1) * n_state]

    def prepare(seq):
        dt = _softplus(dt_ref[seq] + dtb_ref[...])
        acs = _cumsum_rows(dt * (-jnp.exp(alog_ref[...]) * LOG2_E))
        acs_t = acs.T
        dt_t = dt.T
        cb, b_t = [], []
        for g in range(SSM_GROUPS):
            bg = group(b_ref, seq, g)
            cb.append(lax.dot_general(group(c_ref, seq, g), bg, (((1,), (1,)), ((), ())),
                                      preferred_element_type=F32))
            b_t.append(bg.astype(F32).T)
        return dict(
            acs=acs, cb=cb, b_t=b_t,
            e_out=_expand_heads(jnp.exp2(acs), e_ref[...], heads),
            w_state_t=jnp.exp2(acs_t[:, L - 1:L] - acs_t) * dt_t,
            src_t=acs_t - jnp.log2(dt_t))

    def head_pair(seq, p, j):
        g = j // pairs_per_group
        m_blocks, bt_blocks = [], []
        for h in (2 * j, 2 * j + 1):
            seg = p["acs"][:, h:h + 1] - p["src_t"][h:h + 1, :]
            m_blocks.append((p["cb"][g] * jnp.exp2(jnp.where(causal, seg, -jnp.inf))).astype(BF16))
            bt_blocks.append((p["b_t"][g] * p["w_state_t"][h:h + 1, :]).astype(BF16))
        lhs = jnp.concatenate([jnp.concatenate(m_blocks, axis=1), jnp.concatenate(bt_blocks, axis=1)], axis=0)
        xb = xs_ref[seq, :, j * pair:(j + 1) * pair]
        rhs = jnp.concatenate([jnp.where(low_half, xb, 0.0), jnp.where(low_half, 0.0, xb)], axis=0).astype(BF16)
        res = jnp.dot(lhs, rhs, preferred_element_type=F32)
        return res[:L], res[L:]

    def finish(seq, p, y_diag, d_state):
        state = state_ref[seq]
        y_off = [jnp.dot(group(c_ref, seq, g), state[:, g * group_w:(g + 1) * group_w].astype(BF16),
                         preferred_element_type=F32) for g in range(SSM_GROUPS)]
        state_ref[seq] = state * p["e_out"][L - 1:L, :] + jnp.concatenate(d_state, axis=1)
        y = (jnp.concatenate(y_diag, axis=1) + jnp.concatenate(y_off, axis=1) * p["e_out"]
             + xs_ref[seq] * dskip_ref[...])
        y = y * gate_ref[seq]
        normed = []
        for g in range(SSM_GROUPS):
            yg = y[:, g * group_w:(g + 1) * group_w]
            normed.append(yg * lax.rsqrt(jnp.mean(yg * yg, axis=-1, keepdims=True) + EPS))
        o_ref[seq] = (jnp.concatenate(normed, axis=1) * nw_ref[...]).astype(BF16)

    prepared = [prepare(seq) for seq in seqs]
    results = [([], []) for _ in seqs]
    for j in range(n_pairs):
        for seq in seqs:
            y_j, s_j = head_pair(seq, prepared[seq], j)
            results[seq][0].append(y_j)
            results[seq][1].append(s_j)
    for seq in seqs:
        finish(seq, prepared[seq], *results[seq])


def _ssd(gate, xs, b, c, dt_raw, dt_bias, a_log, d_skip, norm_w, batch, seqlen):
    t, d_inner = xs.shape
    gn = b.shape[1]
    heads = d_inner // SSM_HEAD_DIM
    assert 3 * heads <= LANES and batch % SEQS_PER_STEP == 0
    nc = seqlen // CHUNK
    pad = lambda v: jnp.pad(v.reshape(1, heads), ((0, 0), (0, LANES - heads)))
    piece_head = jnp.where(jnp.arange(LANES) < 3 * heads, jnp.arange(LANES) % heads, -1)
    expand = (piece_head[:, None] == (jnp.arange(d_inner) // SSM_HEAD_DIM)[None, :]).astype(BF16)
    seq3 = lambda v: v.reshape(batch, seqlen, v.shape[1])
    blk = lambda n: pl.BlockSpec((SEQS_PER_STEP, CHUNK, n), lambda bi, ci: (bi, ci, 0))
    out = pl.pallas_call(
        _ssd_body,
        grid=(batch // SEQS_PER_STEP, nc),
        in_specs=[blk(d_inner), blk(d_inner), blk(gn), blk(gn), blk(LANES),
                  _resident((1, LANES)), _resident((1, LANES)), _resident((1, d_inner)),
                  _resident((1, d_inner)), _resident((LANES, d_inner))],
        out_specs=blk(d_inner),
        out_shape=jax.ShapeDtypeStruct((batch, seqlen, d_inner), BF16),
        scratch_shapes=[pltpu.VMEM((SEQS_PER_STEP, SSM_STATE, d_inner), F32)],
        compiler_params=_params(("arbitrary", "arbitrary")),
        name="ssd",
    )(seq3(gate), seq3(xs), seq3(b), seq3(c), seq3(dt_raw), pad(dt_bias), pad(a_log),
      jnp.repeat(d_skip, SSM_HEAD_DIM).reshape(1, d_inner), norm_w.reshape(1, d_inner), expand)
    return out.reshape(t, d_inner)


def _key_band(cur_ref, prev_ref, qi, kh):
    W = WINDOW
    parts = []
    for half in range(2):
        cs = slice((2 * kh + half) * LANES, (2 * kh + half + 1) * LANES)
        parts.append(prev_ref[:, cs] if qi == 0 else cur_ref[(qi - 1) * W:qi * W, cs])
        parts.append(cur_ref[qi * W:(qi + 1) * W, cs])
    return jnp.concatenate(parts, axis=0)


def _attn_scores(q_ref, kc_ref, kp_ref, q_blocks):
    W = WINDOW
    blocks_per_kv = q_ref.shape[1] // LANES // N_KV_HEADS
    scores = {}
    for qi in q_blocks:
        rows = slice(qi * W, (qi + 1) * W)
        for kh in range(N_KV_HEADS):
            q_rows = jnp.concatenate([q_ref[rows, (kh * blocks_per_kv + jj) * LANES:(kh * blocks_per_kv + jj + 1) * LANES]
                                      for jj in range(blocks_per_kv)], axis=0)
            scores[qi, kh] = lax.dot_general(q_rows, _key_band(kc_ref, kp_ref, qi, kh), (((1,), (1,)), ((), ())),
                                             preferred_element_type=F32)
    return scores


def _attn_outputs(scores, sink_ref, vc_ref, vp_ref, has_prev, o_ref):
    W = WINDOW
    blocks_per_kv = o_ref.shape[1] // LANES // N_KV_HEADS
    row = lax.broadcasted_iota(jnp.int32, (W, 2 * W), 0)
    col = lax.broadcasted_iota(jnp.int32, (W, 2 * W), 1)
    in_window = (col <= row + W) & (col > row)
    low_half = lax.broadcasted_iota(jnp.int32, (W, LANES), 1) < ATT_HEAD_DIM
    for qi in sorted({qi for qi, _ in scores}):
        rows = slice(qi * W, (qi + 1) * W)
        mask = in_window
        if qi == 0:
            mask = mask & (col >= jnp.where(has_prev, 0, W))
        for kh in range(N_KV_HEADS):
            s = scores[qi, kh]
            weights, inv_sum = [], []
            for jj in range(blocks_per_kv):
                halves, inv = [], []
                for t in range(2):
                    sink = sink_ref[2 * (kh * blocks_per_kv + jj) + t] * LOG2_E
                    sh = jnp.where(mask, s[jj * W:(jj + 1) * W, t * 2 * W:(t + 1) * 2 * W], -jnp.inf)
                    m = jnp.maximum(jnp.max(sh, axis=-1, keepdims=True), sink)
                    p = jnp.exp2(sh - m)
                    inv.append(1.0 / (jnp.sum(p, axis=-1, keepdims=True) + jnp.exp2(sink - m)))
                    halves.append(p.astype(BF16))
                weights.append(jnp.concatenate(halves, axis=1))
                inv_sum.append(jnp.where(low_half, inv[0], inv[1]))
            o = jnp.dot(jnp.concatenate(weights, axis=0), _key_band(vc_ref, vp_ref, qi, kh),
                        preferred_element_type=F32)
            for jj in range(blocks_per_kv):
                j = kh * blocks_per_kv + jj
                o_ref[rows, j * LANES:(j + 1) * LANES] = (o[jj * W:(jj + 1) * W] * inv_sum[jj]).astype(BF16)


def kernel(x, norm_w, ffn_w_gate, ffn_w_up, ffn_w_down, ssm_w_in, ssm_conv_w, ssm_conv_b, ssm_dt_bias, ssm_a_log, ssm_d, ssm_norm_w, ssm_w_out, kv_norm_w, w_k, b_k, w_v, b_v, attn_w_q, attn_b_q, attn_sinks, attn_w_o, attn_b_o, final_norm_w):
    batch, seqlen, d = x.shape
    assert seqlen % TOKEN_TILE == 0 and TOKEN_TILE % (2 * WINDOW) == 0 and seqlen % CHUNK == 0
    assert ssm_w_in.shape[0] == 1 and attn_w_q.shape[0] == 1 and norm_w.shape[0] == 2
    d_inner = ssm_w_out.shape[1]
    heads = d_inner // SSM_HEAD_DIM
    conv_dim = ssm_conv_w.shape[-1]
    bf = lambda w: w.astype(BF16)
    xt = x.reshape(batch * seqlen, d)
    cos, sin = _rope_tables(seqlen)

    ffn_stacks = (ffn_w_gate, ffn_w_up, ffn_w_down)

    (xt,) = _block(xt, norm_w[0, 0], tuple(bf(w[0, 0]) for w in ffn_stacks), ())
    w_in = bf(ssm_w_in[0])
    w_dt = jnp.pad(w_in[:, d_inner + conv_dim:], ((0, 0), (0, LANES - heads)))
    (gate, xs, b, c, dt_raw), ffn_bf = _ssm_in(xt, norm_w[0, 1], w_in, w_dt, ssm_conv_w[0], ssm_conv_b[0], d_inner,
                                               seqlen, cast=tuple(w.reshape(-1, w.shape[-1]) for w in ffn_stacks))
    yn = _ssd(gate, xs, b, c, dt_raw, ssm_dt_bias[0], ssm_a_log[0], ssm_d[0], ssm_norm_w[0], batch, seqlen)
    ffn_weights = tuple(w_bf.reshape(w.shape) for w_bf, w in zip(ffn_bf, ffn_stacks))

    def block(v, layer, idx, **kw):
        return _block(v, norm_w[layer, 2 * idx], ffn_weights, (layer, idx), seqlen=seqlen, **kw)

    xt, k, v = block(xt, 0, 1, pre=(yn, bf(ssm_w_out[0]), None),
                     post=("kv", kv_norm_w, bf(w_k), b_k, bf(w_v), b_v, cos, sin))

    xt, q = block(xt, 1, 0, post=("q", norm_w[1, 1], bf(attn_w_q[0]), attn_b_q[0], cos, sin))
    (xt,) = block(xt, 1, 1, pre=("attn", attn_sinks[0], q, k, v, bf(attn_w_o[0]), attn_b_o[0]),
                  final_nw=final_norm_w)
    return xt.reshape(batch, seqlen, d)
```
